```python
import math
import jax, jax.numpy as jnp
from jax import lax
import numpy as np

D_MODEL = 1024
BATCH = 4
SEQ = 8192
DEPTH = 4

CTX_LEN = 256
GRID_W = 64
HEAD_DIM = 64
SSM_WIDTH = 512
SSM_GROUP = 16
SSM_GROUPS = SSM_WIDTH // SSM_GROUP
SSM_STATE = 64
DT_MIN = 0.001
DT_MAX = 0.1
GQA_HEADS = 8
GQA_KV_HEADS = 2
GQA_WIDTH = GQA_HEADS * HEAD_DIM
GQA_KV_WIDTH = GQA_KV_HEADS * HEAD_DIM
NA_HEADS = 8
NA_WIDTH = NA_HEADS * HEAD_DIM
NA_WIN_H = 8
NA_WIN_W = 16
Q_BLOCK = 128
ROPE_BASE = 10000.0
N_BRANCH = 3
IN_WIDTHS = (SSM_WIDTH, GQA_WIDTH, GQA_KV_WIDTH, GQA_KV_WIDTH, NA_WIDTH, NA_WIDTH, NA_WIDTH, N_BRANCH * D_MODEL)
IN_WIDTH = sum(IN_WIDTHS)
N_EXPERTS = 16
N_EXPERT_GROUPS = 4
EXPERTS_PER_GROUP = N_EXPERTS // N_EXPERT_GROUPS
TOP_K = 2
EXPERT_FF = 512
NORM_EPS = 1e-6

kernel_name = "hybrid_s5_gqa_natten_moe_dit"


def rmsnorm(x, g):
    xf = x.astype(jnp.float32)
    y = xf * lax.rsqrt(jnp.mean(xf * xf, axis=-1, keepdims=True) + NORM_EPS)
    return (y * g.astype(jnp.float32)).astype(x.dtype)


def modulate(h, shift, scale):
    return h * (1.0 + scale) + shift


def axial_rope_tables(n_tokens):
    t = jnp.arange(n_tokens)
    row = (t // GRID_W).astype(jnp.float32)
    col = (t % GRID_W).astype(jnp.float32)
    n_freq = HEAD_DIM // 4
    inv = ROPE_BASE ** (-jnp.arange(n_freq, dtype=jnp.float32) / n_freq)
    ang = jnp.stack([row[:, None] * inv, col[:, None] * inv], axis=1)
    return jnp.cos(ang), jnp.sin(ang)


def apply_axial_rope(x, cos, sin):
    b, s, h, d = x.shape
    xr = x.astype(jnp.float32).reshape(b, s, h, 2, 2, d // 4)
    x1, x2 = xr[..., 0, :], xr[..., 1, :]
    cs, sn = cos[None, :, None], sin[None, :, None]
    out = jnp.stack([x1 * cs - x2 * sn, x2 * cs + x1 * sn], axis=-2)
    return out.reshape(b, s, h, d).astype(x.dtype)


def dense_attend(q, k, v):
    b, lq, hq, d = q.shape
    hk = k.shape[2]
    qg = q.reshape(b, lq, hk, hq // hk, d)
    s = jnp.einsum('bqkgd,bskd->bkgqs', qg, k).astype(jnp.float32) * (d ** -0.5)
    p = jax.nn.softmax(s, axis=-1).astype(v.dtype)
    return jnp.einsum('bkgqs,bskd->bqkgd', p, v).reshape(b, lq, hq * d)


def gqa_latent(q_lat, k_ctx, v_ctx, k_lat, v_lat):
    bsz, n_lat, hq, hd = q_lat.shape
    k_all = jnp.concatenate([k_ctx, k_lat], axis=1)
    v_all = jnp.concatenate([v_ctx, v_lat], axis=1)
    qb = q_lat.reshape(bsz, n_lat // Q_BLOCK, Q_BLOCK, hq, hd).transpose(1, 0, 2, 3, 4)
    out = lax.map(lambda qblk: dense_attend(qblk, k_all, v_all), qb)
    return out.transpose(1, 0, 2, 3).reshape(bsz, n_lat, hq * hd)


def neighborhood_attention(q_lat, k_lat, v_lat, k_ctx, v_ctx, rpb):
    bsz, n_lat, nh, hd = q_lat.shape
    rows = n_lat // GRID_W
    win_h = min(NA_WIN_H, rows)
    scale = hd ** -0.5
    qg = q_lat.reshape(bsz, rows, GRID_W, nh, hd)
    kg = k_lat.reshape(bsz, rows, GRID_W, nh, hd)
    vg = v_lat.reshape(bsz, rows, GRID_W, nh, hd)
    col = jnp.arange(GRID_W)
    col_start = jnp.clip(col - NA_WIN_W // 2, 0, GRID_W - NA_WIN_W)
    col_mask = (col[None, :] >= col_start[:, None]) & (col[None, :] < col_start[:, None] + NA_WIN_W)
    col_off = jnp.clip(col[None, :] - col[:, None], -(NA_WIN_W - 1), NA_WIN_W - 1) + (NA_WIN_W - 1)
    rpb_cols = rpb[:, :, col_off]

    def row_block(args):
        r, q_row = args
        r_start = jnp.clip(r - win_h // 2, 0, rows - win_h)
        kb = lax.dynamic_slice_in_dim(kg, r_start, win_h, axis=1)
        vb = lax.dynamic_slice_in_dim(vg, r_start, win_h, axis=1)
        row_off = r_start + jnp.arange(win_h) - r + (NA_WIN_H - 1)
        bias = rpb_cols[:, row_off].transpose(0, 2, 1, 3)
        s_loc = jnp.einsum('bqhd,bikhd->bhqik', q_row, kb).astype(jnp.float32) * scale + bias[None]
        s_loc = jnp.where(col_mask[None, None, :, None, :], s_loc, -jnp.inf)
        s_loc = s_loc.reshape(bsz, nh, GRID_W, win_h * GRID_W)
        s_ctx = jnp.einsum('bqhd,bchd->bhqc', q_row, k_ctx).astype(jnp.float32) * scale
        p = jax.nn.softmax(jnp.concatenate([s_loc, s_ctx], axis=-1), axis=-1).astype(vb.dtype)
        p_loc = p[..., :win_h * GRID_W].reshape(bsz, nh, GRID_W, win_h, GRID_W)
        p_ctx = p[..., win_h * GRID_W:]
        return (jnp.einsum('bhqik,bikhd->bqhd', p_loc, vb)
                + jnp.einsum('bhqc,bchd->bqhd', p_ctx, v_ctx))

    out = lax.map(row_block, (jnp.arange(rows), qg.transpose(1, 0, 2, 3, 4)))
    return out.transpose(1, 0, 2, 3, 4).reshape(bsz, n_lat, nh * hd)


def _recurrence_combine(left, right):
    a_l, b_l = left
    a_r, b_r = right
    return (a_l * a_r, a_r * b_l + b_r)


def s5_bidirectional(u_ctx, u_lat, lam_re, lam_im, log_dt, b_re, b_im, c_re, c_im, d_skip, need_ctx):
    f32 = jnp.float32
    n_ctx = u_ctx.shape[1]
    n_lat = u_lat.shape[1]

    def scan_direction(seq, d):
        lam = lax.complex(lam_re[d].astype(f32), lam_im[d].astype(f32))
        dt = jnp.exp(log_dt[d].astype(f32))[:, None]
        lam_bar = jnp.exp(lam * dt)
        b_bar = ((lam_bar - 1.0) / lam)[..., None] * lax.complex(b_re[d].astype(f32), b_im[d].astype(f32))
        bsz, length, _ = seq.shape
        ug = seq.astype(f32).reshape(bsz, length, SSM_GROUPS, SSM_GROUP)
        bu = jnp.einsum('blgh,gph->blgp', ug.astype(jnp.complex64), b_bar)
        a = jnp.broadcast_to(lam_bar, (1, length, SSM_GROUPS, SSM_STATE))
        _, states = lax.associative_scan(_recurrence_combine, (a, bu), axis=1)
        return states

    def readout(states, d):
        cm = lax.complex(c_re[d].astype(f32), c_im[d].astype(f32))
        y = jnp.einsum('blgp,ghp->blgh', states, cm).real
        return y.reshape(y.shape[0], y.shape[1], SSM_WIDTH)

    st_f = scan_direction(jnp.concatenate([u_ctx, u_lat], axis=1), 0)
    st_b = jnp.flip(scan_direction(jnp.flip(jnp.concatenate([u_lat, u_ctx], axis=1), axis=1), 1), axis=1)
    dsk = d_skip.astype(f32)
    y_lat = (readout(st_f[:, n_ctx:], 0) + readout(st_b[:, :n_lat], 1)
             + dsk * u_lat.astype(f32)).astype(u_lat.dtype)
    y_ctx = None
    if need_ctx:
        y_ctx = (readout(st_f[:, :n_ctx], 0) + readout(st_b[:, n_lat:], 1)
                 + dsk * u_ctx.astype(f32)).astype(u_ctx.dtype)
    return y_ctx, y_lat


def glu(y, w_glu):
    z = jax.nn.gelu(y)
    return z * jax.nn.sigmoid(z @ w_glu)


def gated_merge(branches, gates, w_branch, w_out):
    b, l, _ = gates.shape
    g = jax.nn.sigmoid(gates.reshape(b, l, N_BRANCH, D_MODEL))
    merged = g[:, :, 0] * (branches[0] @ w_branch[0])
    for i in range(1, N_BRANCH):
        merged = merged + g[:, :, i] * (branches[i] @ w_branch[i])
    return merged @ w_out


def hybrid_mixer(h_ctx, h_lat, cos, sin, w_in, lam_re, lam_im, log_dt, b_re, b_im, c_re, c_im,
                 d_skip, w_glu, q_norm_g, k_norm_g, rpb, w_branch, w_out, need_ctx):
    bsz, n_ctx, _ = h_ctx.shape
    proj = jnp.concatenate([h_ctx, h_lat], axis=1) @ w_in
    splits = [int(v) for v in np.cumsum(IN_WIDTHS)[:-1]]
    u, qa, ka, va, qn, kn, vn, gates = jnp.split(proj, splits, axis=-1)
    cpart = lambda t: t[:, :n_ctx]
    lpart = lambda t: t[:, n_ctx:]
    heads = lambda t, n: t.reshape(t.shape[0], t.shape[1], n, HEAD_DIM)

    ssm_ctx, ssm_lat = s5_bidirectional(cpart(u), lpart(u), lam_re, lam_im, log_dt, b_re, b_im,
                                        c_re, c_im, d_skip, need_ctx)
    qa_h = rmsnorm(heads(qa, GQA_HEADS), q_norm_g)
    ka_h = rmsnorm(heads(ka, GQA_KV_HEADS), k_norm_g)
    va_h = heads(va, GQA_KV_HEADS)
    gqa_lat = gqa_latent(apply_axial_rope(lpart(qa_h), cos, sin), cpart(ka_h), cpart(va_h),
                         apply_axial_rope(lpart(ka_h), cos, sin), lpart(va_h))
    qn_h, kn_h, vn_h = heads(qn, NA_HEADS), heads(kn, NA_HEADS), heads(vn, NA_HEADS)
    na_lat = neighborhood_attention(lpart(qn_h), lpart(kn_h), lpart(vn_h), cpart(kn_h), cpart(vn_h), rpb)

    y_lat = gated_merge([glu(ssm_lat, w_glu), gqa_lat, na_lat], lpart(gates), w_branch, w_out)
    y_ctx = None
    if need_ctx:
        gqa_ctx = dense_attend(cpart(qa_h), cpart(ka_h), cpart(va_h))
        na_ctx = dense_attend(cpart(qn_h), cpart(kn_h), cpart(vn_h))
        y_ctx = gated_merge([glu(ssm_ctx, w_glu), gqa_ctx, na_ctx], cpart(gates), w_branch, w_out)
    return y_ctx, y_lat


def moe_ffn(h, router_w, router_b, w_gate, w_up, w_down):
    probs = jax.nn.softmax((h @ router_w).astype(jnp.float32), axis=-1)
    sel = probs + router_b.astype(jnp.float32)
    grp = sel.reshape(-1, N_EXPERT_GROUPS, EXPERTS_PER_GROUP)
    grp_score = jnp.sum(lax.top_k(grp, TOP_K)[0], axis=-1)
    best = jnp.argmax(grp_score, axis=-1)
    in_grp = (jnp.arange(N_EXPERTS) // EXPERTS_PER_GROUP)[None, :] == best[:, None]
    _, idx = lax.top_k(jnp.where(in_grp, sel, -jnp.inf), TOP_K)
    w_sel = jnp.take_along_axis(probs, idx, axis=-1)
    w_sel = w_sel / jnp.sum(w_sel, axis=-1, keepdims=True)
    combine = jnp.sum(jax.nn.one_hot(idx, N_EXPERTS, dtype=jnp.float32) * w_sel[..., None], axis=1)
    combine = combine.astype(h.dtype)
    out = jnp.zeros_like(h)
    for e in range(N_EXPERTS):
        y_e = (jax.nn.silu(h @ w_gate[e]) * (h @ w_up[e])) @ w_down[e]
        out = out + combine[:, e:e + 1] * y_e
    return out


def setup_inputs(seed: int = 0) -> dict:
    key = jax.random.key(seed)
    ks = jax.random.split(key, 32)
    f32 = jnp.float32
    nrm = lambda k, shape, scale: jax.random.normal(k, shape, f32) * scale
    lam_shape = (DEPTH, 2, SSM_GROUPS, SSM_STATE)
    n_idx = jnp.arange(SSM_STATE, dtype=f32)
    return {
        "x": nrm(ks[0], (BATCH, SEQ, D_MODEL), 1.0),
        "c": nrm(ks[1], (BATCH, D_MODEL), 1.0),
        "ctx": nrm(ks[2], (BATCH, CTX_LEN, D_MODEL), 1.0),
        "c_ctx": nrm(ks[3], (D_MODEL,), 1.0),
        "w_mod": nrm(ks[4], (DEPTH, D_MODEL, 6 * D_MODEL), 0.5 * D_MODEL ** -0.5),
        "b_mod": nrm(ks[5], (DEPTH, 6 * D_MODEL), 0.02),
        "norm1_g": 1.0 + nrm(ks[6], (DEPTH, D_MODEL), 0.02),
        "norm2_g": 1.0 + nrm(ks[7], (DEPTH, D_MODEL), 0.02),
        "w_in": nrm(ks[8], (DEPTH, D_MODEL, IN_WIDTH), D_MODEL ** -0.5),
        "ssm_lam_re": -0.5 * jnp.exp(nrm(ks[9], lam_shape, 0.05)),
        "ssm_lam_im": math.pi * n_idx + nrm(ks[10], lam_shape, 0.01),
        "ssm_log_dt": jax.random.uniform(ks[11], (DEPTH, 2, SSM_GROUPS), f32, math.log(DT_MIN), math.log(DT_MAX)),
        "ssm_b_re": nrm(ks[12], (DEPTH, 2, SSM_GROUPS, SSM_STATE, SSM_GROUP), (2 * SSM_GROUP) ** -0.5),
        "ssm_b_im": nrm(ks[13], (DEPTH, 2, SSM_GROUPS, SSM_STATE, SSM_GROUP), (2 * SSM_GROUP) ** -0.5),
        "ssm_c_re": nrm(ks[14], (DEPTH, 2, SSM_GROUPS, SSM_GROUP, SSM_STATE), SSM_STATE ** -0.5),
        "ssm_c_im": nrm(ks[15], (DEPTH, 2, SSM_GROUPS, SSM_GROUP, SSM_STATE), SSM_STATE ** -0.5),
        "ssm_d": nrm(ks[16], (DEPTH, SSM_WIDTH), 1.0),
        "ssm_w_glu": nrm(ks[17], (DEPTH, SSM_WIDTH, SSM_WIDTH), SSM_WIDTH ** -0.5),
        "gqa_q_norm_g": 1.0 + nrm(ks[18], (DEPTH, HEAD_DIM), 0.02),
        "gqa_k_norm_g": 1.0 + nrm(ks[19], (DEPTH, HEAD_DIM), 0.02),
        "na_rpb": nrm(ks[20], (DEPTH, NA_HEADS, 2 * NA_WIN_H - 1, 2 * NA_WIN_W - 1), 0.1),
        "w_branch": nrm(ks[21], (DEPTH, N_BRANCH, SSM_WIDTH, D_MODEL), SSM_WIDTH ** -0.5),
        "w_out": nrm(ks[22], (DEPTH, D_MODEL, D_MODEL), D_MODEL ** -0.5),
        "router_w": nrm(ks[23], (D_MODEL, N_EXPERTS), D_MODEL ** -0.5),
        "router_b": nrm(ks[24], (N_EXPERTS,), 0.01),
        "moe_w_gate": nrm(ks[25], (DEPTH, N_EXPERTS, D_MODEL, EXPERT_FF), D_MODEL ** -0.5),
        "moe_w_up": nrm(ks[26], (DEPTH, N_EXPERTS, D_MODEL, EXPERT_FF), D_MODEL ** -0.5),
        "moe_w_down": nrm(ks[27], (DEPTH, N_EXPERTS, EXPERT_FF, D_MODEL), EXPERT_FF ** -0.5),
        "final_norm_g": 1.0 + nrm(ks[28], (D_MODEL,), 0.02),
    }


def reference(x, c, ctx, c_ctx, w_mod, b_mod, norm1_g, norm2_g, w_in,
              ssm_lam_re, ssm_lam_im, ssm_log_dt, ssm_b_re, ssm_b_im, ssm_c_re, ssm_c_im,
              ssm_d, ssm_w_glu, gqa_q_norm_g, gqa_k_norm_g, na_rpb, w_branch, w_out,
              router_w, router_b, moe_w_gate, moe_w_up, moe_w_down, final_norm_g):
    bsz, n_lat, d_model = x.shape
    n_ctx = ctx.shape[1]
    cos, sin = axial_rope_tables(n_lat)
    cond_lat = jax.nn.silu(c)
    cond_ctx = jax.nn.silu(c_ctx)
    for layer in range(DEPTH):
        need_ctx = layer < DEPTH - 1
        sh1, sc1, g1, sh2, sc2, g2 = jnp.split((cond_lat @ w_mod[layer] + b_mod[layer])[:, None, :], 6, axis=-1)
        csh1, csc1, cg1, csh2, csc2, cg2 = jnp.split(cond_ctx @ w_mod[layer] + b_mod[layer], 6, axis=-1)
        h_lat = modulate(rmsnorm(x, norm1_g[layer]), sh1, sc1)
        h_ctx = modulate(rmsnorm(ctx, norm1_g[layer]), csh1, csc1)
        y_ctx, y_lat = hybrid_mixer(h_ctx, h_lat, cos, sin, w_in[layer],
                                    ssm_lam_re[layer], ssm_lam_im[layer], ssm_log_dt[layer],
                                    ssm_b_re[layer], ssm_b_im[layer], ssm_c_re[layer], ssm_c_im[layer],
                                    ssm_d[layer], ssm_w_glu[layer], gqa_q_norm_g[layer], gqa_k_norm_g[layer],
                                    na_rpb[layer], w_branch[layer], w_out[layer], need_ctx)
        x = x + g1 * y_lat
        h_lat = modulate(rmsnorm(x, norm2_g[layer]), sh2, sc2).reshape(-1, d_model)
        if need_ctx:
            ctx = ctx + cg1 * y_ctx
            h_ctx = modulate(rmsnorm(ctx, norm2_g[layer]), csh2, csc2).reshape(-1, d_model)
            y = moe_ffn(jnp.concatenate([h_ctx, h_lat], axis=0), router_w, router_b,
                        moe_w_gate[layer], moe_w_up[layer], moe_w_down[layer])
            ctx = ctx + cg2 * y[:bsz * n_ctx].reshape(ctx.shape)
            y = y[bsz * n_ctx:]
        else:
            y = moe_ffn(h_lat, router_w, router_b, moe_w_gate[layer], moe_w_up[layer], moe_w_down[layer])
        x = x + g2 * y.reshape(x.shape)
    return rmsnorm(x, final_norm_g)
```

```python
import functools
import math

import numpy as np
import jax
import jax.numpy as jnp
from jax import lax
from jax.experimental import pallas as pl
from jax.experimental.pallas import tpu as pltpu

D_MODEL = 1024
GRID_W = 64
HEAD_DIM = 64
SSM_WIDTH = 512
SSM_GROUP = 16
SSM_GROUPS = SSM_WIDTH // SSM_GROUP
SSM_STATE = 64
GQA_HEADS = 8
GQA_KV_HEADS = 2
GQA_KV_WIDTH = GQA_KV_HEADS * HEAD_DIM
NA_HEADS = 8
NA_WIDTH = NA_HEADS * HEAD_DIM
NA_WIN_H = 8
NA_WIN_W = 16
ROPE_BASE = 10000.0
N_BRANCH = 3
N_EXPERTS = 16
N_EXPERT_GROUPS = 4
EXPERTS_PER_GROUP = N_EXPERTS // N_EXPERT_GROUPS
EXPERT_FF = 512
NORM_EPS = 1e-6

LANES = 128
SUBLANES = 8
S5_CHUNK = 16
S5_TILE = S5_CHUNK * SSM_GROUP
NA_BLOCK_ROWS = 4
NA_BLOCK = NA_BLOCK_ROWS * GRID_W
NA_SLAB_BLOCKS = 3
NEG_BIG = -1e30
VMEM_LIMIT = 56 * 1024 * 1024

F32 = jnp.float32
BF16 = jnp.bfloat16
HIGHEST = lax.Precision.HIGHEST

IN_WIDTHS = (SSM_WIDTH, GQA_HEADS * LANES, GQA_KV_WIDTH, GQA_KV_WIDTH, NA_WIDTH, NA_WIDTH, NA_WIDTH,
             N_BRANCH * D_MODEL)


def _cparams(*sem):
    return pltpu.CompilerParams(dimension_semantics=sem, vmem_limit_bytes=VMEM_LIMIT)


def _pick_tile(n, cap, mult):
    best = None
    for t in range(mult, min(n, cap) + 1, mult):
        if n % t == 0:
            best = t
    assert best is not None, (n, cap, mult)
    return best


def _dot(a, b):
    return jnp.dot(a, b, preferred_element_type=F32)


def _dot_nt(a, b):
    return lax.dot_general(a, b, (((1,), (1,)), ((), ())), preferred_element_type=F32)


def _dot_tn(a, b):
    return lax.dot_general(a, b, (((0,), (0,)), ((), ())), preferred_element_type=F32)


def _norm_mod(x, g, sc, sh):
    ms = jnp.mean(x * x, axis=-1, keepdims=True)
    return (x * lax.rsqrt(ms + NORM_EPS) * g) * (1.0 + sc) + sh


def _mod_vec(mods_ref, seg, b, row0, tm, n_ctx, n_b):
    cols = slice(seg * D_MODEL, (seg + 1) * D_MODEL)
    if n_ctx % tm == 0:
        r = jnp.where(row0 < n_ctx, n_b, b)
        return mods_ref[pl.ds(r, 1), cols]
    lat = mods_ref[pl.ds(b, 1), cols]
    ctx = mods_ref[pl.ds(n_b, 1), cols]
    rows = row0 + lax.broadcasted_iota(jnp.int32, (tm, 1), 0)
    return jnp.where(rows < n_ctx, ctx, lat)


def _modvec_kernel(c_ref, w_ref, b_ref, o_ref):
    c = c_ref[...]
    s = c * jax.nn.sigmoid(c)
    o_ref[0] = jnp.dot(s, w_ref[0], precision=HIGHEST, preferred_element_type=F32) + b_ref[0]


def _modvecs(cond, w_mod, b_mod):
    depth, d, n = w_mod.shape
    tn = _pick_tile(n, 1536, LANES)
    return pl.pallas_call(
        _modvec_kernel,
        grid=(depth, n // tn),
        in_specs=[pl.BlockSpec((8, d), lambda l, j: (0, 0)),
                  pl.BlockSpec((1, d, tn), lambda l, j: (l, 0, j)),
                  pl.BlockSpec((1, 1, tn), lambda l, j: (l, 0, j))],
        out_specs=pl.BlockSpec((1, 8, tn), lambda l, j: (l, 0, j)),
        out_shape=jax.ShapeDtypeStruct((depth, 8, n), F32),
        compiler_params=_cparams("parallel", "parallel"),
    )(cond, w_mod, b_mod.reshape(depth, 1, n))


def _inproj_kernel(x_ref, mods_ref, g_ref, w_ref, *o_refs, tm, n_ctx, n_b):
    b = pl.program_id(0)
    row0 = pl.program_id(1) * tm
    sh = _mod_vec(mods_ref, 0, b, row0, tm, n_ctx, n_b)
    sc = _mod_vec(mods_ref, 1, b, row0, tm, n_ctx, n_b)
    h = _norm_mod(x_ref[0], g_ref[...], sc, sh).astype(BF16)
    off = 0
    for o_ref, w in zip(o_refs, IN_WIDTHS):
        o_ref[0] = _dot(h, w_ref[:, off:off + w]).astype(o_ref.dtype)
        off += w


def _in_proj(xa, mods, g, w_cat, n_ctx):
    nb, l, d = xa.shape
    tm = _pick_tile(n_ctx, 256, 16)
    n = sum(IN_WIDTHS)
    kern = functools.partial(_inproj_kernel, tm=tm, n_ctx=n_ctx, n_b=nb)
    return pl.pallas_call(
        kern,
        grid=(nb, l // tm),
        in_specs=[pl.BlockSpec((1, tm, d), lambda b, i: (b, i, 0)),
                  pl.BlockSpec(mods.shape, lambda b, i: (0, 0)),
                  pl.BlockSpec((1, d), lambda b, i: (0, 0)),
                  pl.BlockSpec((d, n), lambda b, i: (0, 0))],
        out_specs=[pl.BlockSpec((1, tm, w), lambda b, i: (b, i, 0)) for w in IN_WIDTHS],
        out_shape=[jax.ShapeDtypeStruct((nb, l, w), BF16) for w in IN_WIDTHS],
        compiler_params=_cparams("parallel", "parallel"),
    )(xa, mods, g.reshape(1, d), w_cat)


def _prep_kernel(q_ref, k_ref, cos_ref, sin_ref, gq_ref, gk_ref, qo_ref, ko_ref):
    cos = cos_ref[...]
    sin = sin_ref[...]
    lane = lax.broadcasted_iota(jnp.int32, (1, LANES), 1)
    first = (lane % (HEAD_DIM // 2)) < (HEAD_DIM // 4)
    low = lane < HEAD_DIM

    def rope(x):
        up = pltpu.roll(x, LANES - HEAD_DIM // 4, 1)
        dn = pltpu.roll(x, HEAD_DIM // 4, 1)
        return x * cos + jnp.where(first, up, dn) * sin

    gq = gq_ref[...]
    for h in range(GQA_HEADS):
        x = q_ref[0, :, h * LANES:(h + 1) * LANES].astype(F32)
        ms = jnp.sum(x * x, axis=-1, keepdims=True) * (1.0 / HEAD_DIM)
        y = rope(x * lax.rsqrt(ms + NORM_EPS) * gq) * (HEAD_DIM ** -0.5)
        qo_ref[0, h] = y.astype(qo_ref.dtype)
    x = k_ref[0].astype(F32)
    x2 = x * x
    s_lo = jnp.sum(jnp.where(low, x2, 0.0), axis=-1, keepdims=True)
    s_hi = jnp.sum(jnp.where(low, 0.0, x2), axis=-1, keepdims=True)
    ms = jnp.where(low, s_lo, s_hi) * (1.0 / HEAD_DIM)
    ko_ref[0] = rope(x * lax.rsqrt(ms + NORM_EPS) * gk_ref[...]).astype(ko_ref.dtype)


def _prep_qk(qa, ka, cos_t, sin_t, gq, gk):
    nb, l, _ = qa.shape
    tm = _pick_tile(l, 256, 16)
    tile2 = lambda g: jnp.concatenate([g, g]).reshape(1, LANES).astype(F32)
    return pl.pallas_call(
        _prep_kernel,
        grid=(nb, l // tm),
        in_specs=[pl.BlockSpec((1, tm, GQA_HEADS * LANES), lambda b, i: (b, i, 0)),
                  pl.BlockSpec((1, tm, LANES), lambda b, i: (b, i, 0)),
                  pl.BlockSpec((tm, LANES), lambda b, i: (i, 0)),
                  pl.BlockSpec((tm, LANES), lambda b, i: (i, 0)),
                  pl.BlockSpec((1, LANES), lambda b, i: (0, 0)),
                  pl.BlockSpec((1, LANES), lambda b, i: (0, 0))],
        out_specs=[pl.BlockSpec((1, GQA_HEADS, tm, LANES), lambda b, i: (b, 0, i, 0)),
                   pl.BlockSpec((1, tm, LANES), lambda b, i: (b, i, 0))],
        out_shape=[jax.ShapeDtypeStruct((nb, GQA_HEADS, l, LANES), BF16),
                   jax.ShapeDtypeStruct((nb, l, LANES), BF16)],
        compiler_params=_cparams("parallel", "parallel"),
    )(qa, ka, cos_t, sin_t, tile2(gq), tile2(gk))


def _rope_tables(n_ctx, n_lat):
    t = jnp.arange(n_lat)
    pos = jnp.stack([(t // GRID_W).astype(F32), (t % GRID_W).astype(F32)], axis=1)
    n_freq = HEAD_DIM // 4
    inv = ROPE_BASE ** (-jnp.arange(n_freq, dtype=F32) / n_freq)
    lane = np.arange(LANES)
    axis = (lane % HEAD_DIM) // (HEAD_DIM // 2)
    freq = lane % n_freq
    sign = np.where((lane % (HEAD_DIM // 2)) < n_freq, -1.0, 1.0).astype(np.float32)
    ang = pos[:, axis] * inv[freq][None, :]
    cos_t = jnp.concatenate([jnp.ones((n_ctx, LANES), F32), jnp.cos(ang)], axis=0)
    sin_t = jnp.concatenate([jnp.zeros((n_ctx, LANES), F32), jnp.sin(ang) * sign[None, :]], axis=0)
    return cos_t, sin_t


def _gqa_kernel(q_ref, k_ref, v_ref, o_ref, m_ref, l_ref, acc_ref, *, tq, tk, n_ctx, n_chunks):
    i = pl.program_id(1)
    q = q_ref[0].reshape(GQA_HEADS * tq, LANES)
    m_ref[...] = jnp.full(m_ref.shape, NEG_BIG, F32)
    l_ref[...] = jnp.zeros(l_ref.shape, F32)
    acc_ref[...] = jnp.zeros(acc_ref.shape, F32)

    def step(k, v):
        s = _dot_nt(k, q)
        m_prev = m_ref[...]
        m_new = jnp.maximum(m_prev, jnp.max(s, axis=0, keepdims=True))
        alpha = jnp.exp(m_prev - m_new)
        p = jnp.exp(s - m_new)
        l_ref[...] = alpha * l_ref[...] + jnp.sum(p, axis=0, keepdims=True)
        acc_ref[...] = alpha * acc_ref[...] + _dot_tn(v, p.astype(BF16))
        m_ref[...] = m_new

    step(k_ref[0, 0:n_ctx, :], v_ref[0, 0:n_ctx, :])

    @pl.when(i * tq >= n_ctx)
    def _():
        def body(c, carry):
            st = pl.multiple_of(n_ctx + c * tk, 16)
            step(k_ref[0, pl.ds(st, tk), :], v_ref[0, pl.ds(st, tk), :])
            return carry
        lax.fori_loop(0, n_chunks, body, 0)

    o = acc_ref[...] * (1.0 / l_ref[...])
    per_kv = GQA_HEADS // GQA_KV_HEADS
    pieces = [o[HEAD_DIM * (h // per_kv):HEAD_DIM * (h // per_kv + 1), h * tq:(h + 1) * tq]
              for h in range(GQA_HEADS)]
    o_ref[0] = jnp.concatenate(pieces, axis=0).T.astype(o_ref.dtype)


def _gqa_attention(qp, kp, v, n_ctx):
    nb, _, l, _ = qp.shape
    n_lat = l - n_ctx
    tq = _pick_tile(n_ctx, 256, LANES)
    tk = _pick_tile(n_lat, 512, LANES)
    kern = functools.partial(_gqa_kernel, tq=tq, tk=tk, n_ctx=n_ctx, n_chunks=n_lat // tk)
    return pl.pallas_call(
        kern,
        grid=(nb, l // tq),
        in_specs=[pl.BlockSpec((1, GQA_HEADS, tq, LANES), lambda b, i: (b, 0, i, 0)),
                  pl.BlockSpec((1, l, LANES), lambda b, i: (b, 0, 0)),
                  pl.BlockSpec((1, l, LANES), lambda b, i: (b, 0, 0))],
        out_specs=pl.BlockSpec((1, tq, GQA_HEADS * HEAD_DIM), lambda b, i: (b, i, 0)),
        out_shape=jax.ShapeDtypeStruct((nb, l, GQA_HEADS * HEAD_DIM), BF16),
        scratch_shapes=[pltpu.VMEM((1, GQA_HEADS * tq), F32),
                        pltpu.VMEM((1, GQA_HEADS * tq), F32),
                        pltpu.VMEM((GQA_KV_WIDTH, GQA_HEADS * tq), F32)],
        compiler_params=_cparams("parallel", "arbitrary"),
    )(qp, kp, v)


def _na_kernel(var_ref, q_ref, kc_ref, k0_ref, k1_ref, k2_ref, vc_ref, v0_ref, v1_ref, v2_ref, bm_ref, o_ref):
    del var_ref
    i = pl.program_id(1)
    lane = lax.broadcasted_iota(jnp.int32, (1, LANES), 1)

    def attend(pieces):
        outs = []
        for h in range(NA_HEADS):
            pair, sub = divmod(h, 2)
            cols = slice(pair * LANES, (pair + 1) * LANES)
            own = (lane < HEAD_DIM) if sub == 0 else (lane >= HEAD_DIM)
            qh = (jnp.where(own, q_ref[0, :, cols], 0) * (HEAD_DIM ** -0.5)).astype(BF16)
            scores = []
            for k_ref, _, row0 in pieces:
                s = _dot_nt(k_ref[0, :, cols], qh)
                if row0 is not None:
                    s = s + bm_ref[0, h, row0:row0 + NA_BLOCK, :]
                scores.append(s)
            m = functools.reduce(jnp.maximum, [jnp.max(s, axis=0, keepdims=True) for s in scores])
            den = 0.0
            acc = 0.0
            for s, (_, v_ref, _) in zip(scores, pieces):
                p = jnp.exp(s - m)
                den = den + jnp.sum(p, axis=0, keepdims=True)
                acc = acc + _dot_tn(v_ref[0, :, cols], p.astype(BF16))
            outs.append(acc[sub * HEAD_DIM:(sub + 1) * HEAD_DIM, :] * (1.0 / den))
        o_ref[0] = jnp.concatenate(outs, axis=0).T.astype(o_ref.dtype)

    @pl.when(i == 0)
    def _():
        attend([(kc_ref, vc_ref, None)])

    @pl.when(i > 0)
    def _():
        attend([(k0_ref, v0_ref, 0), (k1_ref, v1_ref, NA_BLOCK), (k2_ref, v2_ref, 2 * NA_BLOCK),
                (kc_ref, vc_ref, None)])


def _na_bias_tables(rpb, n_lat):
    rows = n_lat // GRID_W
    n_blk = n_lat // NA_BLOCK
    slab_rows = NA_SLAB_BLOCKS * NA_BLOCK_ROWS
    assert rows >= slab_rows and rows >= NA_WIN_H
    sigs, var_of_blk = [], []
    for ib in range(n_blk):
        r0 = ib * NA_BLOCK_ROWS
        us = NA_BLOCK_ROWS * min(max(ib - 1, 0), n_blk - NA_SLAB_BLOCKS)
        rs = [min(max(r0 + g - NA_WIN_H // 2, 0), rows - NA_WIN_H) for g in range(NA_BLOCK_ROWS)]
        assert us <= min(rs) and max(rs) + NA_WIN_H <= us + slab_rows
        sig = (us - r0, tuple(r - r0 for r in rs))
        if sig not in sigs:
            sigs.append(sig)
        var_of_blk.append(sigs.index(sig))
    i_k = np.arange(slab_rows)[:, None, None, None]
    c_k = np.arange(GRID_W)[None, :, None, None]
    g_q = np.arange(NA_BLOCK_ROWS)[None, None, :, None]
    c_q = np.arange(GRID_W)[None, None, None, :]
    cs = np.clip(c_q - NA_WIN_W // 2, 0, GRID_W - NA_WIN_W)
    col_ok = (c_k >= cs) & (c_k < cs + NA_WIN_W)
    col_off = np.clip(c_k - c_q, -(NA_WIN_W - 1), NA_WIN_W - 1) + (NA_WIN_W - 1)
    ro_l, co_l, ok_l = [], [], []
    full = (slab_rows, GRID_W, NA_BLOCK_ROWS, GRID_W)
    for du, drs in sigs:
        drs = np.asarray(drs)[None, None, :, None]
        row_ok = (du + i_k >= drs) & (du + i_k < drs + NA_WIN_H)
        row_off = np.clip(du + i_k - g_q + (NA_WIN_H - 1), 0, 2 * NA_WIN_H - 2)
        ro_l.append(np.broadcast_to(row_off, full).reshape(slab_rows * GRID_W, NA_BLOCK))
        co_l.append(np.broadcast_to(col_off, full).reshape(slab_rows * GRID_W, NA_BLOCK))
        ok_l.append(np.broadcast_to(row_ok & col_ok, full).reshape(slab_rows * GRID_W, NA_BLOCK))
    ro, co, ok = np.stack(ro_l), np.stack(co_l), np.stack(ok_l)
    bias = rpb.astype(F32)[:, ro, co]
    bm = jnp.where(ok[None], bias, NEG_BIG).transpose(1, 0, 2, 3)
    return bm, jnp.asarray([0] + var_of_blk, jnp.int32)


def _na_attention(qn, kn, vn, bm, var_idx, n_ctx):
    nb, l, w = qn.shape
    assert n_ctx == NA_BLOCK
    n_lat_blk = (l - n_ctx) // NA_BLOCK
    blk = lambda f: pl.BlockSpec((1, NA_BLOCK, w), f)
    slab = lambda d: blk(lambda b, i, var: (b, 1 + jnp.clip(i - 2, 0, n_lat_blk - NA_SLAB_BLOCKS) + d, 0))
    ctx = blk(lambda b, i, var: (b, 0, 0))
    grid_spec = pltpu.PrefetchScalarGridSpec(
        num_scalar_prefetch=1,
        grid=(nb, l // NA_BLOCK),
        in_specs=[blk(lambda b, i, var: (b, i, 0)),
                  ctx, slab(0), slab(1), slab(2),
                  ctx, slab(0), slab(1), slab(2),
                  pl.BlockSpec((1,) + bm.shape[1:], lambda b, i, var: (var[i], 0, 0, 0))],
        out_specs=blk(lambda b, i, var: (b, i, 0)),
    )
    return pl.pallas_call(
        _na_kernel,
        grid_spec=grid_spec,
        out_shape=jax.ShapeDtypeStruct((nb, l, w), BF16),
        compiler_params=_cparams("parallel", "arbitrary"),
    )(var_idx, qn, kn, kn, kn, kn, vn, vn, vn, vn, bm)


def _s5_kernel(u_ref, t_ref, w_ref, vf_ref, vb_ref, a_ref, o_ref, xl_ref, xf_ref, xb_ref, *, nb, nc, ncc):
    u = u_ref[0]
    xl_ref[...] = _dot(u, w_ref[0])
    ar = a_ref[0, 0:1, :]
    ai = a_ref[0, 1:2, :]
    fwd = lax.broadcasted_iota(jnp.int32, (1, LANES), 1) < SSM_STATE
    re = slice(0, LANES)
    im = slice(LANES, 2 * LANES)

    sub = lax.broadcasted_iota(jnp.int32, (SUBLANES, 1), 0)
    nt, nct = nc // SUBLANES, ncc // SUBLANES

    def body(it, carry):
        tb = jnp.where(it < nct, nct - 1 - it, nt - 1 - (it - nct))
        new = []
        for b in range(nb):
            r, m = carry[2 * b], carry[2 * b + 1]
            rows_f = pl.ds(pl.multiple_of(b * nc + it * SUBLANES, SUBLANES), SUBLANES)
            rows_b = pl.ds(pl.multiple_of(b * nc + tb * SUBLANES, SUBLANES), SUBLANES)
            loc_f = xl_ref[rows_f, :]
            loc_b = xl_ref[rows_b, :]
            ent_f = ent_b = None
            for j in range(SUBLANES):
                jb = SUBLANES - 1 - j
                state = jnp.concatenate([r, m], axis=1)
                ent_f = state if j == 0 else jnp.where(sub == j, state, ent_f)
                ent_b = state if j == 0 else jnp.where(sub == jb, state, ent_b)
                lr = jnp.where(fwd, loc_f[j:j + 1, re], loc_b[jb:jb + 1, re])
                li = jnp.where(fwd, loc_f[j:j + 1, im], loc_b[jb:jb + 1, im])
                r, m = ar * r - ai * m + lr, ar * m + ai * r + li
            xf_ref[rows_f, :] = jnp.broadcast_to(ent_f, (SUBLANES, 2 * LANES))
            xb_ref[rows_b, :] = jnp.broadcast_to(ent_b, (SUBLANES, 2 * LANES))
            new += [r, m]
        return tuple(new)

    zero = jnp.zeros((1, LANES), F32)
    lax.fori_loop(0, nt, body, (zero,) * (2 * nb))
    y = (_dot(u, t_ref[0]) + _dot(xf_ref[...].astype(BF16), vf_ref[0])
         + _dot(xb_ref[...].astype(BF16), vb_ref[0]))
    o_ref[0] = y.astype(o_ref.dtype)


def _s5_tables(lam_re, lam_im, log_dt, b_re, b_im, c_re, c_im, d_skip):
    hp = dict(precision=HIGHEST)
    dt = jnp.exp(log_dt.astype(F32))[:, :, None]
    lr, li = lam_re.astype(F32), lam_im.astype(F32)
    k = jnp.arange(S5_CHUNK + 1, dtype=F32)[:, None, None, None]
    mag = jnp.exp(k * (lr * dt)[None])
    pr = mag * jnp.cos(k * (li * dt)[None])
    pi = mag * jnp.sin(k * (li * dt)[None])
    den = lr * lr + li * li
    zr = ((pr[1] - 1.0) * lr + pi[1] * li) / den
    zi = (pi[1] * lr - (pr[1] - 1.0) * li) / den
    br, bi = b_re.astype(F32), b_im.astype(F32)
    bbr = zr[..., None] * br - zi[..., None] * bi
    bbi = zr[..., None] * bi + zi[..., None] * br
    cr, ci = c_re.astype(F32), c_im.astype(F32)
    clr = cr[None] * pr[:, :, :, None, :] - ci[None] * pi[:, :, :, None, :]
    cli = cr[None] * pi[:, :, :, None, :] + ci[None] * pr[:, :, :, None, :]
    kern = (jnp.einsum('kdgop,dgpi->kdgoi', clr, bbr, **hp)
            - jnp.einsum('kdgop,dgpi->kdgoi', cli, bbi, **hp))
    s_i = np.arange(S5_CHUNK)[:, None]
    t_i = np.arange(S5_CHUNK)[None, :]
    lag = t_i - s_i
    kf = jnp.where((lag >= 0)[:, :, None, None, None], kern[np.abs(lag), 0], 0.0)
    kb = jnp.where((lag <= 0)[:, :, None, None, None], kern[np.abs(lag), 1], 0.0)
    eye_h = jnp.eye(SSM_GROUP, dtype=F32)
    dsk = d_skip.astype(F32).reshape(SSM_GROUPS, SSM_GROUP)
    skip = (jnp.eye(S5_CHUNK, dtype=F32)[:, :, None, None, None]
            * (dsk[:, :, None] * eye_h[None])[None, None])
    tmat = (kf + kb + skip).transpose(2, 0, 4, 1, 3).reshape(SSM_GROUPS, S5_TILE, S5_TILE)
    pf_r, pf_i = pr[S5_CHUNK - 1 - np.arange(S5_CHUNK), 0], pi[S5_CHUNK - 1 - np.arange(S5_CHUNK), 0]
    pb_r, pb_i = pr[np.arange(S5_CHUNK), 1], pi[np.arange(S5_CHUNK), 1]

    def state_in(p_r, p_i, d):
        w_r = p_r[:, :, :, None] * bbr[d][None] - p_i[:, :, :, None] * bbi[d][None]
        w_i = p_r[:, :, :, None] * bbi[d][None] + p_i[:, :, :, None] * bbr[d][None]
        to_cols = lambda w: w.transpose(1, 0, 3, 2).reshape(SSM_GROUPS, S5_TILE, SSM_STATE)
        return to_cols(w_r), to_cols(w_i)

    wf_r, wf_i = state_in(pf_r, pf_i, 0)
    wb_r, wb_i = state_in(pb_r, pb_i, 1)
    wmat = jnp.concatenate([wf_r, wb_r, wf_i, wb_i], axis=-1)
    t_f = np.arange(S5_CHUNK) + 1
    t_b = S5_CHUNK - np.arange(S5_CHUNK)
    to_rows = lambda m: m.transpose(1, 3, 0, 2).reshape(SSM_GROUPS, SSM_STATE, S5_TILE)
    zeros = jnp.zeros((SSM_GROUPS, SSM_STATE, S5_TILE), F32)
    vf = jnp.concatenate([to_rows(clr[t_f, 0]), zeros, -to_rows(cli[t_f, 0]), zeros], axis=1)
    vb = jnp.concatenate([zeros, to_rows(clr[t_b, 1]), zeros, -to_rows(cli[t_b, 1])], axis=1)
    a_re = jnp.concatenate([pr[S5_CHUNK, 0], pr[S5_CHUNK, 1]], axis=-1)
    a_im = jnp.concatenate([pi[S5_CHUNK, 0], pi[S5_CHUNK, 1]], axis=-1)
    amat = jnp.concatenate([a_re[:, None], a_im[:, None], jnp.zeros((SSM_GROUPS, 6, LANES), F32)], axis=1)
    return tmat.astype(BF16), wmat.astype(BF16), vf.astype(BF16), vb.astype(BF16), amat


def _s5(u, tables, n_ctx):
    nb, l, _ = u.shape
    nc = l // S5_CHUNK
    assert nc % SUBLANES == 0 and (n_ctx // S5_CHUNK) % SUBLANES == 0
    ug = (u.reshape(nb, nc, S5_CHUNK, SSM_GROUPS, SSM_GROUP).transpose(3, 0, 1, 2, 4)
          .reshape(SSM_GROUPS, nb * nc, S5_TILE))
    tmat, wmat, vf, vb, amat = tables
    mat = pl.BlockSpec((1, S5_TILE, S5_TILE), lambda g: (g, 0, 0))
    rows = pl.BlockSpec((1, nb * nc, S5_TILE), lambda g: (g, 0, 0))
    kern = functools.partial(_s5_kernel, nb=nb, nc=nc, ncc=n_ctx // S5_CHUNK)
    yg = pl.pallas_call(
        kern,
        grid=(SSM_GROUPS,),
        in_specs=[rows, mat, mat, mat, mat, pl.BlockSpec((1, 8, LANES), lambda g: (g, 0, 0))],
        out_specs=rows,
        out_shape=jax.ShapeDtypeStruct((SSM_GROUPS, nb * nc, S5_TILE), BF16),
        scratch_shapes=[pltpu.VMEM((nb * nc, S5_TILE), F32)] * 3,
        compiler_params=_cparams("parallel"),
    )(ug, tmat, wmat, vf, vb, amat)
    return (yg.reshape(SSM_GROUPS, nb, nc, S5_CHUNK, SSM_GROUP).transpose(1, 2, 3, 0, 4)
            .reshape(nb, l, SSM_WIDTH))


def _merge_kernel(y_ref, a_ref, n_ref, gt_ref, x_ref, mods_ref, wglu_ref, wbr_ref, wout_ref, o_ref,
                  *, tm, n_ctx, n_b):
    b = pl.program_id(0)
    row0 = pl.program_id(1) * tm
    z = jax.nn.gelu(y_ref[0].astype(F32))
    glu = z * jax.nn.sigmoid(_dot(z.astype(BF16), wglu_ref[...]))
    gate = lambda k: jax.nn.sigmoid(gt_ref[0, :, k * D_MODEL:(k + 1) * D_MODEL].astype(F32))
    merged = gate(0) * _dot(glu.astype(BF16), wbr_ref[0])
    merged = merged + gate(1) * _dot(a_ref[0], wbr_ref[1])
    merged = merged + gate(2) * _dot(n_ref[0], wbr_ref[2])
    y = _dot(merged.astype(BF16), wout_ref[...])
    g1 = _mod_vec(mods_ref, 2, b, row0, tm, n_ctx, n_b)
    o_ref[0] = x_ref[0] + g1 * y


def _merge(y_ssm, gqa, na, gates, xa, mods, w_glu, w_branch, w_out, n_ctx):
    nb, l, d = xa.shape
    tm = _pick_tile(n_ctx, 256, 16)
    tok = lambda w: pl.BlockSpec((1, tm, w), lambda b, i: (b, i, 0))
    full = lambda a: pl.BlockSpec(a.shape, lambda b, i: (0,) * a.ndim)
    kern = functools.partial(_merge_kernel, tm=tm, n_ctx=n_ctx, n_b=nb)
    return pl.pallas_call(
        kern,
        grid=(nb, l // tm),
        in_specs=[tok(SSM_WIDTH), tok(SSM_WIDTH), tok(NA_WIDTH), tok(N_BRANCH * d), tok(d),
                  full(mods), full(w_glu), full(w_branch), full(w_out)],
        out_specs=tok(d),
        out_shape=jax.ShapeDtypeStruct((nb, l, d), F32),
        compiler_params=_cparams("parallel", "parallel"),
    )(y_ssm, gqa, na, gates, xa, mods, w_glu, w_branch, w_out)


def _top2_sum(v):
    hi1, lo1 = jnp.maximum(v[0], v[1]), jnp.minimum(v[0], v[1])
    hi2, lo2 = jnp.maximum(v[2], v[3]), jnp.minimum(v[2], v[3])
    return jnp.maximum(hi1, hi2) + jnp.maximum(jnp.minimum(hi1, hi2), jnp.maximum(lo1, lo2))


def _first_argmax(vals):
    best, idx = vals[0], jnp.zeros(vals[0].shape, jnp.int32)
    for j in range(1, len(vals)):
        upd = vals[j] > best
        best = jnp.where(upd, vals[j], best)
        idx = jnp.where(upd, j, idx)
    return best, idx


def _router_kernel(x_ref, mods_ref, g_ref, rw_ref, rb_ref, o_ref, *, tm, n_ctx, n_b):
    b = pl.program_id(0)
    row0 = pl.program_id(1) * tm
    sh = _mod_vec(mods_ref, 3, b, row0, tm, n_ctx, n_b)
    sc = _mod_vec(mods_ref, 4, b, row0, tm, n_ctx, n_b)
    h = _norm_mod(x_ref[0], g_ref[...], sc, sh)
    logits = lax.dot_general(rw_ref[...], h, (((1,), (1,)), ((), ())), precision=HIGHEST,
                             preferred_element_type=F32)
    rows = [logits[e:e + 1, :] for e in range(N_EXPERTS)]
    mx = functools.reduce(jnp.maximum, rows)
    ex = [jnp.exp(r - mx) for r in rows]
    den = functools.reduce(lambda a, c: a + c, ex)
    probs = [e / den for e in ex]
    sel = [probs[e] + rb_ref[e:e + 1, :] for e in range(N_EXPERTS)]
    grp = [sel[g * EXPERTS_PER_GROUP:(g + 1) * EXPERTS_PER_GROUP] for g in range(N_EXPERT_GROUPS)]
    _, best = _first_argmax([_top2_sum(v) for v in grp])
    pick = lambda table, j: functools.reduce(
        lambda a, g: jnp.where(best == g, table[g * EXPERTS_PER_GROUP + j], a),
        range(1, N_EXPERT_GROUPS), table[j])
    sel_g = [pick(sel, j) for j in range(EXPERTS_PER_GROUP)]
    prob_g = [pick(probs, j) for j in range(EXPERTS_PER_GROUP)]
    _, i1 = _first_argmax(sel_g)
    _, i2 = _first_argmax([jnp.where(i1 == j, -jnp.inf, sel_g[j]) for j in range(EXPERTS_PER_GROUP)])
    at = lambda idx: functools.reduce(lambda a, j: jnp.where(idx == j, prob_g[j], a),
                                      range(1, EXPERTS_PER_GROUP), prob_g[0])
    w1, w2 = at(i1), at(i2)
    tot = w1 + w2
    w1, w2 = w1 / tot, w2 / tot
    out = []
    for e in range(N_EXPERTS):
        g, j = divmod(e, EXPERTS_PER_GROUP)
        c = jnp.where(i1 == j, w1, 0.0) + jnp.where(i2 == j, w2, 0.0)
        out.append(jnp.where(best == g, c, 0.0))
    o_ref[0] = jnp.concatenate(out, axis=0)


def _router(xa, mods, g, router_w, router_b, n_ctx):
    nb, l, d = xa.shape
    tm = _pick_tile(n_ctx, 256, LANES)
    kern = functools.partial(_router_kernel, tm=tm, n_ctx=n_ctx, n_b=nb)
    comb_t = pl.pallas_call(
        kern,
        grid=(nb, l // tm),
        in_specs=[pl.BlockSpec((1, tm, d), lambda b, i: (b, i, 0)),
                  pl.BlockSpec(mods.shape, lambda b, i: (0, 0)),
                  pl.BlockSpec((1, d), lambda b, i: (0, 0)),
                  pl.BlockSpec((N_EXPERTS, d), lambda b, i: (0, 0)),
                  pl.BlockSpec((N_EXPERTS, 1), lambda b, i: (0, 0))],
        out_specs=pl.BlockSpec((1, N_EXPERTS, tm), lambda b, i: (b, 0, i)),
        out_shape=jax.ShapeDtypeStruct((nb, N_EXPERTS, l), F32),
        compiler_params=_cparams("parallel", "parallel"),
    )(xa, mods, g.reshape(1, d), router_w.T.astype(F32), router_b.reshape(N_EXPERTS, 1).astype(F32))
    return comb_t.transpose(0, 2, 1)


def _moe_kernel(x_ref, comb_ref, mods_ref, g_ref, wg_ref, wu_ref, wd_ref, o_ref, h_ref, acc_ref,
                *, tm, n_ctx, n_b):
    b = pl.program_id(0)
    row0 = pl.program_id(1) * tm
    e = pl.program_id(2)

    @pl.when(e == 0)
    def _():
        sh = _mod_vec(mods_ref, 3, b, row0, tm, n_ctx, n_b)
        sc = _mod_vec(mods_ref, 4, b, row0, tm, n_ctx, n_b)
        h_ref[...] = _norm_mod(x_ref[0], g_ref[...], sc, sh).astype(BF16)
        acc_ref[...] = jnp.zeros(acc_ref.shape, F32)

    h = h_ref[...]
    gate = _dot(h, wg_ref[0])
    up = _dot(h, wu_ref[0])
    lane = lax.broadcasted_iota(jnp.int32, (1, N_EXPERTS), 1)
    ce = jnp.sum(jnp.where(lane == e, comb_ref[0], 0.0), axis=1, keepdims=True)
    act = (gate * jax.nn.sigmoid(gate)) * up * ce
    acc_ref[...] += _dot(act.astype(BF16), wd_ref[0])

    @pl.when(e == N_EXPERTS - 1)
    def _():
        g2 = _mod_vec(mods_ref, 5, b, row0, tm, n_ctx, n_b)
        o_ref[0] = x_ref[0] + g2 * acc_ref[...]


def _moe(xa, comb, mods, g, w_gate, w_up, w_down, n_ctx):
    nb, l, d = xa.shape
    tm = _pick_tile(l, 1536, LANES)
    kern = functools.partial(_moe_kernel, tm=tm, n_ctx=n_ctx, n_b=nb)
    return pl.pallas_call(
        kern,
        grid=(nb, l // tm, N_EXPERTS),
        in_specs=[pl.BlockSpec((1, tm, d), lambda b, i, e: (b, i, 0)),
                  pl.BlockSpec((1, tm, N_EXPERTS), lambda b, i, e: (b, i, 0)),
                  pl.BlockSpec(mods.shape, lambda b, i, e: (0, 0)),
                  pl.BlockSpec((1, d), lambda b, i, e: (0, 0)),
                  pl.BlockSpec((1, d, EXPERT_FF), lambda b, i, e: (e, 0, 0)),
                  pl.BlockSpec((1, d, EXPERT_FF), lambda b, i, e: (e, 0, 0)),
                  pl.BlockSpec((1, EXPERT_FF, d), lambda b, i, e: (e, 0, 0))],
        out_specs=pl.BlockSpec((1, tm, d), lambda b, i, e: (b, i, 0)),
        out_shape=jax.ShapeDtypeStruct((nb, l, d), F32),
        scratch_shapes=[pltpu.VMEM((tm, d), BF16), pltpu.VMEM((tm, d), F32)],
        compiler_params=_cparams("parallel", "parallel", "arbitrary"),
    )(xa, comb, mods, g.reshape(1, d), w_gate, w_up, w_down)


def _final_norm_kernel(x_ref, g_ref, o_ref):
    x = x_ref[0]
    ms = jnp.mean(x * x, axis=-1, keepdims=True)
    o_ref[0] = x * lax.rsqrt(ms + NORM_EPS) * g_ref[...]


def _final_norm(xa, g, n_ctx):
    nb, l, d = xa.shape
    tm = _pick_tile(n_ctx, 256, 8)
    off = n_ctx // tm
    return pl.pallas_call(
        _final_norm_kernel,
        grid=(nb, (l - n_ctx) // tm),
        in_specs=[pl.BlockSpec((1, tm, d), lambda b, i: (b, i + off, 0)),
                  pl.BlockSpec((1, d), lambda b, i: (0, 0))],
        out_specs=pl.BlockSpec((1, tm, d), lambda b, i: (b, i, 0)),
        out_shape=jax.ShapeDtypeStruct((nb, l - n_ctx, d), F32),
        compiler_params=_cparams("parallel", "parallel"),
    )(xa, g.reshape(1, d).astype(F32))


def _in_proj_weights(w_in):
    w_u, w_qa, w_rest = jnp.split(w_in, [SSM_WIDTH, SSM_WIDTH + GQA_HEADS * HEAD_DIM], axis=-1)
    d = w_in.shape[0]
    per_kv = GQA_HEADS // GQA_KV_HEADS
    tiles = []
    for h in range(GQA_HEADS):
        w_h = w_qa[:, h * HEAD_DIM:(h + 1) * HEAD_DIM]
        zero = jnp.zeros((d, HEAD_DIM), w_in.dtype)
        tiles += [w_h, zero] if h // per_kv == 0 else [zero, w_h]
    return jnp.concatenate([w_u] + tiles + [w_rest], axis=-1).astype(BF16)


def kernel(x, c, ctx, c_ctx, w_mod, b_mod, norm1_g, norm2_g, w_in, ssm_lam_re, ssm_lam_im, ssm_log_dt, ssm_b_re, ssm_b_im, ssm_c_re, ssm_c_im, ssm_d, ssm_w_glu, gqa_q_norm_g, gqa_k_norm_g, na_rpb, w_branch, w_out, router_w, router_b, moe_w_gate, moe_w_up, moe_w_down, final_norm_g):
    nb, n_lat, d = x.shape
    n_ctx = ctx.shape[1]
    depth = w_mod.shape[0]
    assert d == D_MODEL and nb < 8 and GQA_KV_HEADS == 2 and sum(IN_WIDTHS) % LANES == 0
    assert n_ctx % S5_CHUNK == 0 and n_lat % NA_BLOCK == 0

    xa = jnp.concatenate([ctx, x], axis=1).astype(F32)
    cond = jnp.zeros((8, d), F32).at[:nb].set(c.astype(F32)).at[nb].set(c_ctx.astype(F32))
    mods = _modvecs(cond, w_mod.astype(F32), b_mod.astype(F32))
    cos_t, sin_t = _rope_tables(n_ctx, n_lat)

    for layer in range(depth):
        u, qa, ka, va, qn, kn, vn, gates = _in_proj(xa, mods[layer], norm1_g[layer].astype(F32),
                                                    _in_proj_weights(w_in[layer]), n_ctx)
        qp, kp = _prep_qk(qa, ka, cos_t, sin_t, gqa_q_norm_g[layer], gqa_k_norm_g[layer])
        tables = _s5_tables(ssm_lam_re[layer], ssm_lam_im[layer], ssm_log_dt[layer], ssm_b_re[layer],
                            ssm_b_im[layer], ssm_c_re[layer], ssm_c_im[layer], ssm_d[layer])
        y_ssm = _s5(u, tables, n_ctx)
        gqa = _gqa_attention(qp, kp, va, n_ctx)
        bm, var_idx = _na_bias_tables(na_rpb[layer], n_lat)
        na = _na_attention(qn, kn, vn, bm, var_idx, n_ctx)
        xa = _merge(y_ssm, gqa, na, gates, xa, mods[layer], ssm_w_glu[layer].astype(BF16),
                    w_branch[layer].astype(BF16), w_out[layer].astype(BF16), n_ctx)
        comb = _router(xa, mods[layer], norm2_g[layer].astype(F32), router_w, router_b, n_ctx)
        xa = _moe(xa, comb, mods[layer], norm2_g[layer].astype(F32), moe_w_gate[layer].astype(BF16),
                  moe_w_up[layer].astype(BF16), moe_w_down[layer].astype(BF16), n_ctx)
    return _final_norm(xa, final_norm_g, n_ctx)
```

```python
import functools
import math

import numpy as np
import jax
import jax.numpy as jnp
from jax import lax
from jax.experimental import pallas as pl
from jax.experimental.pallas import tpu as pltpu

D_MODEL = 1024
GRID_W = 64
HEAD_DIM = 64
SSM_WIDTH = 512
SSM_GROUP = 16
SSM_GROUPS = SSM_WIDTH // SSM_GROUP
SSM_STATE = 64
GQA_HEADS = 8
GQA_KV_HEADS = 2
GQA_KV_WIDTH = GQA_KV_HEADS * HEAD_DIM
NA_HEADS = 8
NA_WIDTH = NA_HEADS * HEAD_DIM
NA_WIN_H = 8
NA_WIN_W = 16
ROPE_BASE = 10000.0
N_BRANCH = 3
N_EXPERTS = 16
N_EXPERT_GROUPS = 4
EXPERTS_PER_GROUP = N_EXPERTS // N_EXPERT_GROUPS
EXPERT_FF = 512
NORM_EPS = 1e-6

LANES = 128
SUBLANES = 8
S5_CHUNK = 16
S5_TILE = S5_CHUNK * SSM_GROUP
NA_BLOCK_ROWS = 4
NA_BLOCK = NA_BLOCK_ROWS * GRID_W
NA_SLAB_BLOCKS = 3
NEG_BIG = -1e30
LOG2_E = math.log2(math.e)
VMEM_LIMIT = 56 * 1024 * 1024

F32 = jnp.float32
BF16 = jnp.bfloat16
HIGHEST = lax.Precision.HIGHEST

IN_WIDTHS = (SSM_WIDTH, GQA_HEADS * LANES, GQA_KV_WIDTH, GQA_KV_WIDTH, NA_WIDTH, NA_WIDTH, NA_WIDTH,
             N_BRANCH * D_MODEL)


def _cparams(*sem):
    return pltpu.CompilerParams(dimension_semantics=sem, vmem_limit_bytes=VMEM_LIMIT)


def _pick_tile(n, cap, mult):
    best = None
    for t in range(mult, min(n, cap) + 1, mult):
        if n % t == 0:
            best = t
    assert best is not None, (n, cap, mult)
    return best


def _dot(a, b):
    return jnp.dot(a, b, preferred_element_type=F32)


def _dot_nt(a, b):
    return lax.dot_general(a, b, (((1,), (1,)), ((), ())), preferred_element_type=F32)


def _dot_tn(a, b):
    return lax.dot_general(a, b, (((0,), (0,)), ((), ())), preferred_element_type=F32)


def _norm_mod(x, g, sc, sh):
    ms = jnp.mean(x * x, axis=-1, keepdims=True)
    return (x * lax.rsqrt(ms + NORM_EPS) * g) * (1.0 + sc) + sh


def _mod_vec(mods_ref, seg, b, row0, tm, n_ctx, n_b):
    cols = slice(seg * D_MODEL, (seg + 1) * D_MODEL)
    if n_ctx % tm == 0:
        r = jnp.where(row0 < n_ctx, n_b, b)
        return mods_ref[pl.ds(r, 1), cols]
    lat = mods_ref[pl.ds(b, 1), cols]
    ctx = mods_ref[pl.ds(n_b, 1), cols]
    rows = row0 + lax.broadcasted_iota(jnp.int32, (tm, 1), 0)
    return jnp.where(rows < n_ctx, ctx, lat)


def _modvec_kernel(c_ref, w_ref, b_ref, o_ref):
    c = c_ref[...]
    s = c * jax.nn.sigmoid(c)
    o_ref[0] = jnp.dot(s, w_ref[0], precision=HIGHEST, preferred_element_type=F32) + b_ref[0]


def _modvecs(cond, w_mod, b_mod):
    depth, d, n = w_mod.shape
    tn = _pick_tile(n, 1536, LANES)
    return pl.pallas_call(
        _modvec_kernel,
        grid=(depth, n // tn),
        in_specs=[pl.BlockSpec((8, d), lambda l, j: (0, 0)),
                  pl.BlockSpec((1, d, tn), lambda l, j: (l, 0, j)),
                  pl.BlockSpec((1, 1, tn), lambda l, j: (l, 0, j))],
        out_specs=pl.BlockSpec((1, 8, tn), lambda l, j: (l, 0, j)),
        out_shape=jax.ShapeDtypeStruct((depth, 8, n), F32),
        compiler_params=_cparams("parallel", "parallel"),
    )(cond, w_mod, b_mod.reshape(depth, 1, n))


def _inproj_kernel(x_ref, mods_ref, g_ref, w_ref, *o_refs, tm, n_ctx, n_b):
    b = pl.program_id(0)
    row0 = pl.program_id(1) * tm
    sh = _mod_vec(mods_ref, 0, b, row0, tm, n_ctx, n_b)
    sc = _mod_vec(mods_ref, 1, b, row0, tm, n_ctx, n_b)
    h = _norm_mod(x_ref[0], g_ref[...], sc, sh).astype(BF16)
    off = 0
    for o_ref, w in zip(o_refs, IN_WIDTHS):
        o_ref[0] = _dot(h, w_ref[:, off:off + w]).astype(o_ref.dtype)
        off += w


def _in_proj(xa, mods, g, w_cat, n_ctx):
    nb, l, d = xa.shape
    tm = _pick_tile(n_ctx, 256, 16)
    n = sum(IN_WIDTHS)
    kern = functools.partial(_inproj_kernel, tm=tm, n_ctx=n_ctx, n_b=nb)
    return pl.pallas_call(
        kern,
        grid=(nb, l // tm),
        in_specs=[pl.BlockSpec((1, tm, d), lambda b, i: (b, i, 0)),
                  pl.BlockSpec(mods.shape, lambda b, i: (0, 0)),
                  pl.BlockSpec((1, d), lambda b, i: (0, 0)),
                  pl.BlockSpec((d, n), lambda b, i: (0, 0))],
        out_specs=[pl.BlockSpec((1, tm, w), lambda b, i: (b, i, 0)) for w in IN_WIDTHS],
        out_shape=[jax.ShapeDtypeStruct((nb, l, w), BF16) for w in IN_WIDTHS],
        compiler_params=_cparams("parallel", "parallel"),
    )(xa, mods, g.reshape(1, d), w_cat)


def _prep_kernel(q_ref, k_ref, v_ref, cos_ref, sin_ref, gq_ref, gk_ref, qo_ref, ko_ref, vo_ref):
    cos = cos_ref[...]
    sin = sin_ref[...]
    lane = lax.broadcasted_iota(jnp.int32, (1, LANES), 1)
    first = (lane % (HEAD_DIM // 2)) < (HEAD_DIM // 4)
    low = lane < HEAD_DIM

    def rope(x):
        up = pltpu.roll(x, LANES - HEAD_DIM // 4, 1)
        dn = pltpu.roll(x, HEAD_DIM // 4, 1)
        return x * cos + jnp.where(first, up, dn) * sin

    gq = gq_ref[...] * (HEAD_DIM ** -0.5 * LOG2_E)
    for h in range(GQA_HEADS):
        x = q_ref[0, :, h * LANES:(h + 1) * LANES].astype(F32)
        ms = jnp.sum(x * x, axis=-1, keepdims=True) * (1.0 / HEAD_DIM)
        qo_ref[0, h] = rope(x * lax.rsqrt(ms + NORM_EPS) * gq).astype(qo_ref.dtype)
    x = k_ref[0].astype(F32)
    x2 = x * x
    s_lo = jnp.sum(jnp.where(low, x2, 0.0), axis=-1, keepdims=True)
    s_hi = jnp.sum(jnp.where(low, 0.0, x2), axis=-1, keepdims=True)
    ms = jnp.where(low, s_lo, s_hi) * (1.0 / HEAD_DIM)
    ko_ref[0] = rope(x * lax.rsqrt(ms + NORM_EPS) * gk_ref[...]).astype(ko_ref.dtype)
    v = v_ref[0]
    one = jnp.ones(v.shape, v.dtype)
    vo_ref[0, 0] = jnp.where(low, v, one)
    vo_ref[0, 1] = jnp.where(low, one, v)


def _prep_qkv(qa, ka, va, cos_t, sin_t, gq, gk):
    nb, l, _ = qa.shape
    tm = _pick_tile(l, 256, 16)
    tile2 = lambda g: jnp.concatenate([g, g]).reshape(1, LANES).astype(F32)
    tok = pl.BlockSpec((1, tm, LANES), lambda b, i: (b, i, 0))
    tab = pl.BlockSpec((tm, LANES), lambda b, i: (i, 0))
    vec = pl.BlockSpec((1, LANES), lambda b, i: (0, 0))
    return pl.pallas_call(
        _prep_kernel,
        grid=(nb, l // tm),
        in_specs=[pl.BlockSpec((1, tm, GQA_HEADS * LANES), lambda b, i: (b, i, 0)), tok, tok, tab, tab, vec, vec],
        out_specs=[pl.BlockSpec((1, GQA_HEADS, tm, LANES), lambda b, i: (b, 0, i, 0)), tok,
                   pl.BlockSpec((1, GQA_KV_HEADS, tm, LANES), lambda b, i: (b, 0, i, 0))],
        out_shape=[jax.ShapeDtypeStruct((nb, GQA_HEADS, l, LANES), BF16),
                   jax.ShapeDtypeStruct((nb, l, LANES), BF16),
                   jax.ShapeDtypeStruct((nb, GQA_KV_HEADS, l, LANES), BF16)],
        compiler_params=_cparams("parallel", "parallel"),
    )(qa, ka, va, cos_t, sin_t, tile2(gq), tile2(gk))


def _rope_tables(n_ctx, n_lat):
    t = jnp.arange(n_lat)
    pos = jnp.stack([(t // GRID_W).astype(F32), (t % GRID_W).astype(F32)], axis=1)
    n_freq = HEAD_DIM // 4
    inv = ROPE_BASE ** (-jnp.arange(n_freq, dtype=F32) / n_freq)
    lane = np.arange(LANES)
    axis = (lane % HEAD_DIM) // (HEAD_DIM // 2)
    freq = lane % n_freq
    sign = np.where((lane % (HEAD_DIM // 2)) < n_freq, -1.0, 1.0).astype(np.float32)
    ang = pos[:, axis] * inv[freq][None, :]
    cos_t = jnp.concatenate([jnp.ones((n_ctx, LANES), F32), jnp.cos(ang)], axis=0)
    sin_t = jnp.concatenate([jnp.zeros((n_ctx, LANES), F32), jnp.sin(ang) * sign[None, :]], axis=0)
    return cos_t, sin_t


def _gqa_kernel(q_ref, k_ref, v_ref, o_ref, s0_ref, s1_ref, p0_ref, p1_ref, acc_ref, *, tq, tk, n_ctx, n_chunks):
    i = pl.program_id(1)
    per_kv = GQA_HEADS // GQA_KV_HEADS
    half = per_kv * tq
    q = q_ref[0].reshape(GQA_HEADS * tq, LANES)
    s_slots, p_slots = (s0_ref, s1_ref), (p0_ref, p1_ref)

    def scores(rows, n, s_ref):
        s = _dot_nt(k_ref[0, rows, :], q)
        s_ref[0:n, :] = s
        return jnp.max(s, axis=0, keepdims=True)

    def softmax(n, s_ref, p_ref, m_prev, c_max):
        m_new = jnp.maximum(m_prev, c_max)
        p_ref[0:n, :] = jnp.exp2(s_ref[0:n, :] - m_new).astype(BF16)
        return m_new, jnp.exp2(m_prev - m_new)

    def weighted_values(rows, n, p_ref, alpha):
        for j in range(GQA_KV_HEADS):
            cols = slice(j * half, (j + 1) * half)
            upd = _dot_tn(v_ref[0, j, rows, :], p_ref[0:n, cols])
            acc_ref[:, cols] = upd if alpha is None else alpha[:, cols] * acc_ref[:, cols] + upd

    def finalize():
        pieces = []
        for h in range(GQA_HEADS):
            cols = slice(h * tq, (h + 1) * tq)
            j = h // per_kv
            den = acc_ref[(1 - j) * HEAD_DIM:(1 - j) * HEAD_DIM + 1, cols]
            pieces.append(acc_ref[j * HEAD_DIM:(j + 1) * HEAD_DIM, cols] * (1.0 / den))
        o_ref[0] = jnp.concatenate(pieces, axis=0).T.astype(o_ref.dtype)

    m_init = jnp.full((1, GQA_HEADS * tq), NEG_BIG, F32)

    @pl.when(i * tq < n_ctx)
    def _():
        rows = slice(0, n_ctx)
        c_max = scores(rows, n_ctx, s0_ref)
        softmax(n_ctx, s0_ref, p0_ref, m_init, c_max)
        weighted_values(rows, n_ctx, p0_ref, None)
        finalize()

    @pl.when(i * tq >= n_ctx)
    def _():
        chunk = lambda c: slice(c * tk, (c + 1) * tk)
        m, alpha, c_max = m_init, {}, {}
        c_max[0] = scores(chunk(0), tk, s_slots[0])
        for c in range(n_chunks):
            if c + 1 < n_chunks:
                c_max[c + 1] = scores(chunk(c + 1), tk, s_slots[(c + 1) % 2])
            m, alpha[c] = softmax(tk, s_slots[c % 2], p_slots[c % 2], m, c_max[c])
            if c >= 1:
                weighted_values(chunk(c - 1), tk, p_slots[(c - 1) % 2], alpha[c - 1] if c > 1 else None)
        last = n_chunks - 1
        weighted_values(chunk(last), tk, p_slots[last % 2], alpha[last] if last > 0 else None)
        finalize()


def _gqa_attention(qp, kp, vp, n_ctx):
    nb, _, l, _ = qp.shape
    tq = _pick_tile(n_ctx, 256, LANES)
    tk = _pick_tile(l, 768, LANES)
    assert n_ctx <= tk
    kern = functools.partial(_gqa_kernel, tq=tq, tk=tk, n_ctx=n_ctx, n_chunks=l // tk)
    return pl.pallas_call(
        kern,
        grid=(nb, l // tq),
        in_specs=[pl.BlockSpec((1, GQA_HEADS, tq, LANES), lambda b, i: (b, 0, i, 0)),
                  pl.BlockSpec((1, l, LANES), lambda b, i: (b, 0, 0)),
                  pl.BlockSpec((1, GQA_KV_HEADS, l, LANES), lambda b, i: (b, 0, 0, 0))],
        out_specs=pl.BlockSpec((1, tq, GQA_HEADS * HEAD_DIM), lambda b, i: (b, i, 0)),
        out_shape=jax.ShapeDtypeStruct((nb, l, GQA_HEADS * HEAD_DIM), BF16),
        scratch_shapes=[pltpu.VMEM((tk, GQA_HEADS * tq), F32), pltpu.VMEM((tk, GQA_HEADS * tq), F32),
                        pltpu.VMEM((tk, GQA_HEADS * tq), BF16), pltpu.VMEM((tk, GQA_HEADS * tq), BF16),
                        pltpu.VMEM((GQA_KV_WIDTH, GQA_HEADS * tq), F32)],
        compiler_params=_cparams("parallel", "arbitrary"),
    )(qp, kp, vp)


def _na_kernel(var_ref, q_ref, kc_ref, k0_ref, k1_ref, k2_ref, vc_ref, v0_ref, v1_ref, v2_ref, bm_ref, o_ref):
    del var_ref
    i = pl.program_id(1)
    lane = lax.broadcasted_iota(jnp.int32, (1, LANES), 1)

    def attend(pieces):
        def head_scores(h):
            pair, sub = divmod(h, 2)
            cols = slice(pair * LANES, (pair + 1) * LANES)
            own = (lane < HEAD_DIM) if sub == 0 else (lane >= HEAD_DIM)
            qh = (jnp.where(own, q_ref[0, :, cols], 0) * (HEAD_DIM ** -0.5)).astype(BF16)
            scores = []
            for k_ref, _, row0 in pieces:
                s = _dot_nt(k_ref[0, :, cols], qh)
                if row0 is not None:
                    s = s + bm_ref[0, h, row0:row0 + NA_BLOCK, :]
                scores.append(s)
            return scores

        def head_out(h, scores):
            pair, sub = divmod(h, 2)
            cols = slice(pair * LANES, (pair + 1) * LANES)
            m = functools.reduce(jnp.maximum, [jnp.max(s, axis=0, keepdims=True) for s in scores])
            den = 0.0
            acc = 0.0
            for s, (_, v_ref, _) in zip(scores, pieces):
                p = jnp.exp(s - m)
                den = den + jnp.sum(p, axis=0, keepdims=True)
                acc = acc + _dot_tn(v_ref[0, :, cols], p.astype(BF16))
            return acc[sub * HEAD_DIM:(sub + 1) * HEAD_DIM, :] * (1.0 / den)

        outs = []
        scores = head_scores(0)
        for h in range(NA_HEADS):
            nxt = head_scores(h + 1) if h + 1 < NA_HEADS else None
            outs.append(head_out(h, scores))
            scores = nxt
        o_ref[0] = jnp.concatenate(outs, axis=0).T.astype(o_ref.dtype)

    @pl.when(i == 0)
    def _():
        attend([(kc_ref, vc_ref, None)])

    @pl.when(i > 0)
    def _():
        attend([(k0_ref, v0_ref, 0), (k1_ref, v1_ref, NA_BLOCK), (k2_ref, v2_ref, 2 * NA_BLOCK),
                (kc_ref, vc_ref, None)])


def _na_bias_tables(rpb, n_lat):
    rows = n_lat // GRID_W
    n_blk = n_lat // NA_BLOCK
    slab_rows = NA_SLAB_BLOCKS * NA_BLOCK_ROWS
    assert rows >= slab_rows and rows >= NA_WIN_H
    sigs, var_of_blk = [], []
    for ib in range(n_blk):
        r0 = ib * NA_BLOCK_ROWS
        us = NA_BLOCK_ROWS * min(max(ib - 1, 0), n_blk - NA_SLAB_BLOCKS)
        rs = [min(max(r0 + g - NA_WIN_H // 2, 0), rows - NA_WIN_H) for g in range(NA_BLOCK_ROWS)]
        assert us <= min(rs) and max(rs) + NA_WIN_H <= us + slab_rows
        sig = (us - r0, tuple(r - r0 for r in rs))
        if sig not in sigs:
            sigs.append(sig)
        var_of_blk.append(sigs.index(sig))
    i_k = np.arange(slab_rows)[:, None, None, None]
    c_k = np.arange(GRID_W)[None, :, None, None]
    g_q = np.arange(NA_BLOCK_ROWS)[None, None, :, None]
    c_q = np.arange(GRID_W)[None, None, None, :]
    cs = np.clip(c_q - NA_WIN_W // 2, 0, GRID_W - NA_WIN_W)
    col_ok = (c_k >= cs) & (c_k < cs + NA_WIN_W)
    col_off = np.clip(c_k - c_q, -(NA_WIN_W - 1), NA_WIN_W - 1) + (NA_WIN_W - 1)
    n_co = 2 * NA_WIN_W - 1
    onehot = (col_off[0, :, 0, :][None] == np.arange(n_co)[:, None, None]).astype(np.float32)
    toe = jnp.einsum('hrc,ckq->hrkq', rpb.astype(F32), onehot, precision=HIGHEST)
    full = (slab_rows, GRID_W, NA_BLOCK_ROWS, GRID_W)
    tables = []
    for du, drs in sigs:
        drs = np.asarray(drs)[None, None, :, None]
        row_ok = (du + i_k >= drs) & (du + i_k < drs + NA_WIN_H)
        ok = np.broadcast_to(row_ok & col_ok, full).reshape(slab_rows * GRID_W, NA_BLOCK)
        row_off = np.clip(du + i_k[:, 0, :, 0] - g_q[0, 0, :, 0][None, :] + (NA_WIN_H - 1), 0, 2 * NA_WIN_H - 2)
        bias = toe[:, row_off].transpose(0, 1, 3, 2, 4)
        tables.append(jnp.where(ok[None], bias.reshape(NA_HEADS, slab_rows * GRID_W, NA_BLOCK), NEG_BIG))
    return jnp.stack(tables), jnp.asarray([0] + var_of_blk, jnp.int32)


def _na_attention(qn, kn, vn, bm, var_idx, n_ctx):
    nb, l, w = qn.shape
    assert n_ctx == NA_BLOCK
    n_lat_blk = (l - n_ctx) // NA_BLOCK
    blk = lambda f: pl.BlockSpec((1, NA_BLOCK, w), f)
    slab = lambda d: blk(lambda b, i, var: (b, 1 + jnp.clip(i - 2, 0, n_lat_blk - NA_SLAB_BLOCKS) + d, 0))
    ctx = blk(lambda b, i, var: (b, 0, 0))
    grid_spec = pltpu.PrefetchScalarGridSpec(
        num_scalar_prefetch=1,
        grid=(nb, l // NA_BLOCK),
        in_specs=[blk(lambda b, i, var: (b, i, 0)),
                  ctx, slab(0), slab(1), slab(2),
                  ctx, slab(0), slab(1), slab(2),
                  pl.BlockSpec((1,) + bm.shape[1:], lambda b, i, var: (var[i], 0, 0, 0))],
        out_specs=blk(lambda b, i, var: (b, i, 0)),
    )
    return pl.pallas_call(
        _na_kernel,
        grid_spec=grid_spec,
        out_shape=jax.ShapeDtypeStruct((nb, l, w), BF16),
        compiler_params=_cparams("parallel", "arbitrary"),
    )(var_idx, qn, kn, kn, kn, kn, vn, vn, vn, vn, bm)


def _s5_kernel(u_ref, t_ref, w_ref, vf_ref, vb_ref, a_ref, o_ref, xl_ref, xf_ref, xb_ref, *, nb, nc, ncc):
    u = u_ref[0]
    xl_ref[...] = _dot(u, w_ref[0])
    ar = a_ref[0, 0:1, :]
    ai = a_ref[0, 1:2, :]
    fwd = lax.broadcasted_iota(jnp.int32, (1, LANES), 1) < SSM_STATE
    re = slice(0, LANES)
    im = slice(LANES, 2 * LANES)

    sub = lax.broadcasted_iota(jnp.int32, (SUBLANES, 1), 0)
    nt, nct = nc // SUBLANES, ncc // SUBLANES

    def body(it, carry):
        tb = jnp.where(it < nct, nct - 1 - it, nt - 1 - (it - nct))
        new = []
        for b in range(nb):
            r, m = carry[2 * b], carry[2 * b + 1]
            rows_f = pl.ds(pl.multiple_of(b * nc + it * SUBLANES, SUBLANES), SUBLANES)
            rows_b = pl.ds(pl.multiple_of(b * nc + tb * SUBLANES, SUBLANES), SUBLANES)
            loc_f = xl_ref[rows_f, :]
            loc_b = xl_ref[rows_b, :]
            ent_f = ent_b = None
            for j in range(SUBLANES):
                jb = SUBLANES - 1 - j
                state = jnp.concatenate([r, m], axis=1)
                ent_f = state if j == 0 else jnp.where(sub == j, state, ent_f)
                ent_b = state if j == 0 else jnp.where(sub == jb, state, ent_b)
                lr = jnp.where(fwd, loc_f[j:j + 1, re], loc_b[jb:jb + 1, re])
                li = jnp.where(fwd, loc_f[j:j + 1, im], loc_b[jb:jb + 1, im])
                r, m = ar * r - ai * m + lr, ar * m + ai * r + li
            xf_ref[rows_f, :] = jnp.broadcast_to(ent_f, (SUBLANES, 2 * LANES))
            xb_ref[rows_b, :] = jnp.broadcast_to(ent_b, (SUBLANES, 2 * LANES))
            new += [r, m]
        return tuple(new)

    zero = jnp.zeros((1, LANES), F32)
    lax.fori_loop(0, nt, body, (zero,) * (2 * nb))
    y = (_dot(u, t_ref[0]) + _dot(xf_ref[...].astype(BF16), vf_ref[0])
         + _dot(xb_ref[...].astype(BF16), vb_ref[0]))
    o_ref[0] = y.astype(o_ref.dtype)


def _s5_tables(lam_re, lam_im, log_dt, b_re, b_im, c_re, c_im, d_skip):
    hp = dict(precision=HIGHEST)
    dt = jnp.exp(log_dt.astype(F32))[:, :, None]
    lr, li = lam_re.astype(F32), lam_im.astype(F32)
    k = jnp.arange(S5_CHUNK + 1, dtype=F32)[:, None, None, None]
    mag = jnp.exp(k * (lr * dt)[None])
    pr = mag * jnp.cos(k * (li * dt)[None])
    pi = mag * jnp.sin(k * (li * dt)[None])
    den = lr * lr + li * li
    zr = ((pr[1] - 1.0) * lr + pi[1] * li) / den
    zi = (pi[1] * lr - (pr[1] - 1.0) * li) / den
    br, bi = b_re.astype(F32), b_im.astype(F32)
    bbr = zr[..., None] * br - zi[..., None] * bi
    bbi = zr[..., None] * bi + zi[..., None] * br
    cr, ci = c_re.astype(F32), c_im.astype(F32)
    clr = cr[None] * pr[:, :, :, None, :] - ci[None] * pi[:, :, :, None, :]
    cli = cr[None] * pi[:, :, :, None, :] + ci[None] * pr[:, :, :, None, :]
    kern = (jnp.einsum('kdgop,dgpi->kdgoi', clr, bbr, **hp)
            - jnp.einsum('kdgop,dgpi->kdgoi', cli, bbi, **hp))
    s_i = np.arange(S5_CHUNK)[:, None]
    t_i = np.arange(S5_CHUNK)[None, :]
    lag = t_i - s_i
    kf = jnp.where((lag >= 0)[:, :, None, None, None], kern[np.abs(lag), 0], 0.0)
    kb = jnp.where((lag <= 0)[:, :, None, None, None], kern[np.abs(lag), 1], 0.0)
    eye_h = jnp.eye(SSM_GROUP, dtype=F32)
    dsk = d_skip.astype(F32).reshape(SSM_GROUPS, SSM_GROUP)
    skip = (jnp.eye(S5_CHUNK, dtype=F32)[:, :, None, None, None]
            * (dsk[:, :, None] * eye_h[None])[None, None])
    tmat = (kf + kb + skip).transpose(2, 0, 4, 1, 3).reshape(SSM_GROUPS, S5_TILE, S5_TILE)
    pf_r, pf_i = pr[S5_CHUNK - 1 - np.arange(S5_CHUNK), 0], pi[S5_CHUNK - 1 - np.arange(S5_CHUNK), 0]
    pb_r, pb_i = pr[np.arange(S5_CHUNK), 1], pi[np.arange(S5_CHUNK), 1]

    def state_in(p_r, p_i, d):
        w_r = p_r[:, :, :, None] * bbr[d][None] - p_i[:, :, :, None] * bbi[d][None]
        w_i = p_r[:, :, :, None] * bbi[d][None] + p_i[:, :, :, None] * bbr[d][None]
        to_cols = lambda w: w.transpose(1, 0, 3, 2).reshape(SSM_GROUPS, S5_TILE, SSM_STATE)
        return to_cols(w_r), to_cols(w_i)

    wf_r, wf_i = state_in(pf_r, pf_i, 0)
    wb_r, wb_i = state_in(pb_r, pb_i, 1)
    wmat = jnp.concatenate([wf_r, wb_r, wf_i, wb_i], axis=-1)
    t_f = np.arange(S5_CHUNK) + 1
    t_b = S5_CHUNK - np.arange(S5_CHUNK)
    to_rows = lambda m: m.transpose(1, 3, 0, 2).reshape(SSM_GROUPS, SSM_STATE, S5_TILE)
    zeros = jnp.zeros((SSM_GROUPS, SSM_STATE, S5_TILE), F32)
    vf = jnp.concatenate([to_rows(clr[t_f, 0]), zeros, -to_rows(cli[t_f, 0]), zeros], axis=1)
    vb = jnp.concatenate([zeros, to_rows(clr[t_b, 1]), zeros, -to_rows(cli[t_b, 1])], axis=1)
    a_re = jnp.concatenate([pr[S5_CHUNK, 0], pr[S5_CHUNK, 1]], axis=-1)
    a_im = jnp.concatenate([pi[S5_CHUNK, 0], pi[S5_CHUNK, 1]], axis=-1)
    amat = jnp.concatenate([a_re[:, None], a_im[:, None], jnp.zeros((SSM_GROUPS, 6, LANES), F32)], axis=1)
    return tmat.astype(BF16), wmat.astype(BF16), vf.astype(BF16), vb.astype(BF16), amat


def _s5(u, tables, n_ctx):
    nb, l, _ = u.shape
    nc = l // S5_CHUNK
    assert nc % SUBLANES == 0 and (n_ctx // S5_CHUNK) % SUBLANES == 0
    ug = (u.reshape(nb, nc, S5_CHUNK, SSM_GROUPS, SSM_GROUP).transpose(3, 0, 1, 2, 4)
          .reshape(SSM_GROUPS, nb * nc, S5_TILE))
    tmat, wmat, vf, vb, amat = tables
    mat = pl.BlockSpec((1, S5_TILE, S5_TILE), lambda g: (g, 0, 0))
    rows = pl.BlockSpec((1, nb * nc, S5_TILE), lambda g: (g, 0, 0))
    kern = functools.partial(_s5_kernel, nb=nb, nc=nc, ncc=n_ctx // S5_CHUNK)
    yg = pl.pallas_call(
        kern,
        grid=(SSM_GROUPS,),
        in_specs=[rows, mat, mat, mat, mat, pl.BlockSpec((1, 8, LANES), lambda g: (g, 0, 0))],
        out_specs=rows,
        out_shape=jax.ShapeDtypeStruct((SSM_GROUPS, nb * nc, S5_TILE), BF16),
        scratch_shapes=[pltpu.VMEM((nb * nc, S5_TILE), F32)] * 3,
        compiler_params=_cparams("parallel"),
    )(ug, tmat, wmat, vf, vb, amat)
    return (yg.reshape(SSM_GROUPS, nb, nc, S5_CHUNK, SSM_GROUP).transpose(1, 2, 3, 0, 4)
            .reshape(nb, l, SSM_WIDTH))


def _merge_kernel(y_ref, a_ref, n_ref, gt_ref, x_ref, mods_ref, wglu_ref, wbr_ref, wout_ref, o_ref,
                  *, tm, n_ctx, n_b):
    b = pl.program_id(0)
    row0 = pl.program_id(1) * tm
    z = jax.nn.gelu(y_ref[0].astype(F32))
    glu = z * jax.nn.sigmoid(_dot(z.astype(BF16), wglu_ref[...]))
    gate = lambda k: jax.nn.sigmoid(gt_ref[0, :, k * D_MODEL:(k + 1) * D_MODEL].astype(F32))
    merged = gate(0) * _dot(glu.astype(BF16), wbr_ref[0])
    merged = merged + gate(1) * _dot(a_ref[0], wbr_ref[1])
    merged = merged + gate(2) * _dot(n_ref[0], wbr_ref[2])
    y = _dot(merged.astype(BF16), wout_ref[...])
    g1 = _mod_vec(mods_ref, 2, b, row0, tm, n_ctx, n_b)
    o_ref[0] = x_ref[0] + g1 * y


def _merge(y_ssm, gqa, na, gates, xa, mods, w_glu, w_branch, w_out, n_ctx):
    nb, l, d = xa.shape
    tm = _pick_tile(n_ctx, 256, 16)
    tok = lambda w: pl.BlockSpec((1, tm, w), lambda b, i: (b, i, 0))
    full = lambda a: pl.BlockSpec(a.shape, lambda b, i: (0,) * a.ndim)
    kern = functools.partial(_merge_kernel, tm=tm, n_ctx=n_ctx, n_b=nb)
    return pl.pallas_call(
        kern,
        grid=(nb, l // tm),
        in_specs=[tok(SSM_WIDTH), tok(SSM_WIDTH), tok(NA_WIDTH), tok(N_BRANCH * d), tok(d),
                  full(mods), full(w_glu), full(w_branch), full(w_out)],
        out_specs=tok(d),
        out_shape=jax.ShapeDtypeStruct((nb, l, d), F32),
        compiler_params=_cparams("parallel", "parallel"),
    )(y_ssm, gqa, na, gates, xa, mods, w_glu, w_branch, w_out)


def _top2_sum(v):
    hi1, lo1 = jnp.maximum(v[0], v[1]), jnp.minimum(v[0], v[1])
    hi2, lo2 = jnp.maximum(v[2], v[3]), jnp.minimum(v[2], v[3])
    return jnp.maximum(hi1, hi2) + jnp.maximum(jnp.minimum(hi1, hi2), jnp.maximum(lo1, lo2))


def _first_argmax(vals):
    best, idx = vals[0], jnp.zeros(vals[0].shape, jnp.int32)
    for j in range(1, len(vals)):
        upd = vals[j] > best
        best = jnp.where(upd, vals[j], best)
        idx = jnp.where(upd, j, idx)
    return best, idx


def _router_kernel(x_ref, mods_ref, g_ref, rw_ref, rb_ref, o_ref, *, tm, n_ctx, n_b):
    b = pl.program_id(0)
    row0 = pl.program_id(1) * tm
    sh = _mod_vec(mods_ref, 3, b, row0, tm, n_ctx, n_b)
    sc = _mod_vec(mods_ref, 4, b, row0, tm, n_ctx, n_b)
    h = _norm_mod(x_ref[0], g_ref[...], sc, sh)
    logits = lax.dot_general(rw_ref[...], h, (((1,), (1,)), ((), ())), precision=HIGHEST,
                             preferred_element_type=F32)
    rows = [logits[e:e + 1, :] for e in range(N_EXPERTS)]
    mx = functools.reduce(jnp.maximum, rows)
    ex = [jnp.exp(r - mx) for r in rows]
    den = functools.reduce(lambda a, c: a + c, ex)
    probs = [e / den for e in ex]
    sel = [probs[e] + rb_ref[e:e + 1, :] for e in range(N_EXPERTS)]
    grp = [sel[g * EXPERTS_PER_GROUP:(g + 1) * EXPERTS_PER_GROUP] for g in range(N_EXPERT_GROUPS)]
    _, best = _first_argmax([_top2_sum(v) for v in grp])
    pick = lambda table, j: functools.reduce(
        lambda a, g: jnp.where(best == g, table[g * EXPERTS_PER_GROUP + j], a),
        range(1, N_EXPERT_GROUPS), table[j])
    sel_g = [pick(sel, j) for j in range(EXPERTS_PER_GROUP)]
    prob_g = [pick(probs, j) for j in range(EXPERTS_PER_GROUP)]
    _, i1 = _first_argmax(sel_g)
    _, i2 = _first_argmax([jnp.where(i1 == j, -jnp.inf, sel_g[j]) for j in range(EXPERTS_PER_GROUP)])
    at = lambda idx: functools.reduce(lambda a, j: jnp.where(idx == j, prob_g[j], a),
                                      range(1, EXPERTS_PER_GROUP), prob_g[0])
    w1, w2 = at(i1), at(i2)
    tot = w1 + w2
    w1, w2 = w1 / tot, w2 / tot
    out = []
    for e in range(N_EXPERTS):
        g, j = divmod(e, EXPERTS_PER_GROUP)
        c = jnp.where(i1 == j, w1, 0.0) + jnp.where(i2 == j, w2, 0.0)
        out.append(jnp.where(best == g, c, 0.0))
    o_ref[0] = jnp.concatenate(out, axis=0)


def _router(xa, mods, g, router_w, router_b, n_ctx):
    nb, l, d = xa.shape
    tm = _pick_tile(n_ctx, 256, LANES)
    kern = functools.partial(_router_kernel, tm=tm, n_ctx=n_ctx, n_b=nb)
    comb_t = pl.pallas_call(
        kern,
        grid=(nb, l // tm),
        in_specs=[pl.BlockSpec((1, tm, d), lambda b, i: (b, i, 0)),
                  pl.BlockSpec(mods.shape, lambda b, i: (0, 0)),
                  pl.BlockSpec((1, d), lambda b, i: (0, 0)),
                  pl.BlockSpec((N_EXPERTS, d), lambda b, i: (0, 0)),
                  pl.BlockSpec((N_EXPERTS, 1), lambda b, i: (0, 0))],
        out_specs=pl.BlockSpec((1, N_EXPERTS, tm), lambda b, i: (b, 0, i)),
        out_shape=jax.ShapeDtypeStruct((nb, N_EXPERTS, l), F32),
        compiler_params=_cparams("parallel", "parallel"),
    )(xa, mods, g.reshape(1, d), router_w.T.astype(F32), router_b.reshape(N_EXPERTS, 1).astype(F32))
    return comb_t.transpose(0, 2, 1)


def _moe_kernel(x_ref, comb_ref, mods_ref, g_ref, wg_ref, wu_ref, wd_ref, o_ref, h_ref, acc_ref,
                *, tm, n_ctx, n_b):
    b = pl.program_id(0)
    row0 = pl.program_id(1) * tm
    e = pl.program_id(2)

    @pl.when(e == 0)
    def _():
        sh = _mod_vec(mods_ref, 3, b, row0, tm, n_ctx, n_b)
        sc = _mod_vec(mods_ref, 4, b, row0, tm, n_ctx, n_b)
        h_ref[...] = _norm_mod(x_ref[0], g_ref[...], sc, sh).astype(BF16)
        acc_ref[...] = jnp.zeros(acc_ref.shape, F32)

    h = h_ref[...]
    gate = _dot(h, wg_ref[0])
    up = _dot(h, wu_ref[0])
    lane = lax.broadcasted_iota(jnp.int32, (1, N_EXPERTS), 1)
    ce = jnp.sum(jnp.where(lane == e, comb_ref[0], 0.0), axis=1, keepdims=True)
    act = (gate * jax.nn.sigmoid(gate)) * up * ce
    acc_ref[...] += _dot(act.astype(BF16), wd_ref[0])

    @pl.when(e == N_EXPERTS - 1)
    def _():
        g2 = _mod_vec(mods_ref, 5, b, row0, tm, n_ctx, n_b)
        o_ref[0] = x_ref[0] + g2 * acc_ref[...]


def _moe(xa, comb, mods, g, w_gate, w_up, w_down, n_ctx):
    nb, l, d = xa.shape
    tm = _pick_tile(l, 1536, LANES)
    kern = functools.partial(_moe_kernel, tm=tm, n_ctx=n_ctx, n_b=nb)
    return pl.pallas_call(
        kern,
        grid=(nb, l // tm, N_EXPERTS),
        in_specs=[pl.BlockSpec((1, tm, d), lambda b, i, e: (b, i, 0)),
                  pl.BlockSpec((1, tm, N_EXPERTS), lambda b, i, e: (b, i, 0)),
                  pl.BlockSpec(mods.shape, lambda b, i, e: (0, 0)),
                  pl.BlockSpec((1, d), lambda b, i, e: (0, 0)),
                  pl.BlockSpec((1, d, EXPERT_FF), lambda b, i, e: (e, 0, 0)),
                  pl.BlockSpec((1, d, EXPERT_FF), lambda b, i, e: (e, 0, 0)),
                  pl.BlockSpec((1, EXPERT_FF, d), lambda b, i, e: (e, 0, 0))],
        out_specs=pl.BlockSpec((1, tm, d), lambda b, i, e: (b, i, 0)),
        out_shape=jax.ShapeDtypeStruct((nb, l, d), F32),
        scratch_shapes=[pltpu.VMEM((tm, d), BF16), pltpu.VMEM((tm, d), F32)],
        compiler_params=_cparams("parallel", "parallel", "arbitrary"),
    )(xa, comb, mods, g.reshape(1, d), w_gate, w_up, w_down)


def _final_norm_kernel(x_ref, g_ref, o_ref):
    x = x_ref[0]
    ms = jnp.mean(x * x, axis=-1, keepdims=True)
    o_ref[0] = x * lax.rsqrt(ms + NORM_EPS) * g_ref[...]


def _final_norm(xa, g, n_ctx):
    nb, l, d = xa.shape
    tm = _pick_tile(n_ctx, 256, 8)
    off = n_ctx // tm
    return pl.pallas_call(
        _final_norm_kernel,
        grid=(nb, (l - n_ctx) // tm),
        in_specs=[pl.BlockSpec((1, tm, d), lambda b, i: (b, i + off, 0)),
                  pl.BlockSpec((1, d), lambda b, i: (0, 0))],
        out_specs=pl.BlockSpec((1, tm, d), lambda b, i: (b, i, 0)),
        out_shape=jax.ShapeDtypeStruct((nb, l - n_ctx, d), F32),
        compiler_params=_cparams("parallel", "parallel"),
    )(xa, g.reshape(1, d).astype(F32))


def _in_proj_weights(w_in):
    w_u, w_qa, w_rest = jnp.split(w_in, [SSM_WIDTH, SSM_WIDTH + GQA_HEADS * HEAD_DIM], axis=-1)
    d = w_in.shape[0]
    per_kv = GQA_HEADS // GQA_KV_HEADS
    tiles = []
    for h in range(GQA_HEADS):
        w_h = w_qa[:, h * HEAD_DIM:(h + 1) * HEAD_DIM]
        zero = jnp.zeros((d, HEAD_DIM), w_in.dtype)
        tiles += [w_h, zero] if h // per_kv == 0 else [zero, w_h]
    return jnp.concatenate([w_u] + tiles + [w_rest], axis=-1).astype(BF16)


def kernel(x, c, ctx, c_ctx, w_mod, b_mod, norm1_g, norm2_g, w_in, ssm_lam_re, ssm_lam_im, ssm_log_dt, ssm_b_re, ssm_b_im, ssm_c_re, ssm_c_im, ssm_d, ssm_w_glu, gqa_q_norm_g, gqa_k_norm_g, na_rpb, w_branch, w_out, router_w, router_b, moe_w_gate, moe_w_up, moe_w_down, final_norm_g):
    nb, n_lat, d = x.shape
    n_ctx = ctx.shape[1]
    depth = w_mod.shape[0]
    assert d == D_MODEL and nb < 8 and GQA_KV_HEADS == 2 and sum(IN_WIDTHS) % LANES == 0
    assert n_ctx % S5_CHUNK == 0 and n_lat % NA_BLOCK == 0

    xa = jnp.concatenate([ctx, x], axis=1).astype(F32)
    cond = jnp.zeros((8, d), F32).at[:nb].set(c.astype(F32)).at[nb].set(c_ctx.astype(F32))
    mods = _modvecs(cond, w_mod.astype(F32), b_mod.astype(F32))
    cos_t, sin_t = _rope_tables(n_ctx, n_lat)

    for layer in range(depth):
        u, qa, ka, va, qn, kn, vn, gates = _in_proj(xa, mods[layer], norm1_g[layer].astype(F32),
                                                    _in_proj_weights(w_in[layer]), n_ctx)
        qp, kp, vp = _prep_qkv(qa, ka, va, cos_t, sin_t, gqa_q_norm_g[layer], gqa_k_norm_g[layer])
        tables = _s5_tables(ssm_lam_re[layer], ssm_lam_im[layer], ssm_log_dt[layer], ssm_b_re[layer],
                            ssm_b_im[layer], ssm_c_re[layer], ssm_c_im[layer], ssm_d[layer])
        y_ssm = _s5(u, tables, n_ctx)
        gqa = _gqa_attention(qp, kp, vp, n_ctx)
        bm, var_idx = _na_bias_tables(na_rpb[layer], n_lat)
        na = _na_attention(qn, kn, vn, bm, var_idx, n_ctx)
        xa = _merge(y_ssm, gqa, na, gates, xa, mods[layer], ssm_w_glu[layer].astype(BF16),
                    w_branch[layer].astype(BF16), w_out[layer].astype(BF16), n_ctx)
        comb = _router(xa, mods[layer], norm2_g[layer].astype(F32), router_w, router_b, n_ctx)
        xa = _moe(xa, comb, mods[layer], norm2_g[layer].astype(F32), moe_w_gate[layer].astype(BF16),
                  moe_w_up[layer].astype(BF16), moe_w_down[layer].astype(BF16), n_ctx)
    return _final_norm(xa, final_norm_g, n_ctx)
```

```python
import functools
import math

import numpy as np
import jax
import jax.numpy as jnp
from jax import lax
from jax.experimental import pallas as pl
from jax.experimental.pallas import tpu as pltpu

D_MODEL = 1024
GRID_W = 64
HEAD_DIM = 64
SSM_WIDTH = 512
SSM_GROUP = 16
SSM_GROUPS = SSM_WIDTH // SSM_GROUP
SSM_STATE = 64
GQA_HEADS = 8
GQA_KV_HEADS = 2
GQA_KV_WIDTH = GQA_KV_HEADS * HEAD_DIM
NA_HEADS = 8
NA_WIDTH = NA_HEADS * HEAD_DIM
NA_WIN_H = 8
NA_WIN_W = 16
ROPE_BASE = 10000.0
N_BRANCH = 3
N_EXPERTS = 16
N_EXPERT_GROUPS = 4
EXPERTS_PER_GROUP = N_EXPERTS // N_EXPERT_GROUPS
EXPERT_FF = 512
NORM_EPS = 1e-6

LANES = 128
SUBLANES = 8
S5_CHUNK = 16
S5_TILE = S5_CHUNK * SSM_GROUP
S5_GPT = 8
NA_BLOCK_ROWS = 4
NA_BLOCK = NA_BLOCK_ROWS * GRID_W
MOE_PAIR = 2
MOE_PAIRS = N_EXPERTS // MOE_PAIR
MOE_CAP = 128
NA_SLAB_BLOCKS = 3
NEG_BIG = -1e30
LOG2_E = math.log2(math.e)
VMEM_LIMIT = 56 * 1024 * 1024

F32 = jnp.float32
BF16 = jnp.bfloat16
HIGHEST = lax.Precision.HIGHEST

IN_WIDTHS = (SSM_WIDTH, GQA_HEADS * LANES, GQA_KV_WIDTH, GQA_KV_WIDTH, NA_WIDTH, NA_WIDTH, NA_WIDTH,
             N_BRANCH * D_MODEL)


def _cparams(*sem):
    return pltpu.CompilerParams(dimension_semantics=sem, vmem_limit_bytes=VMEM_LIMIT)


def _pick_tile(n, cap, mult):
    best = None
    for t in range(mult, min(n, cap) + 1, mult):
        if n % t == 0:
            best = t
    assert best is not None, (n, cap, mult)
    return best


def _dot(a, b):
    return jnp.dot(a, b, preferred_element_type=F32)


def _dot_nt(a, b):
    return lax.dot_general(a, b, (((1,), (1,)), ((), ())), preferred_element_type=F32)


def _dot_tn(a, b):
    return lax.dot_general(a, b, (((0,), (0,)), ((), ())), preferred_element_type=F32)


def _norm_mod(x, g, sc, sh):
    ms = jnp.mean(x * x, axis=-1, keepdims=True)
    return (x * lax.rsqrt(ms + NORM_EPS) * g) * (1.0 + sc) + sh


def _mod_vec(mods_ref, seg, b, row0, tm, n_ctx, n_b):
    cols = slice(seg * D_MODEL, (seg + 1) * D_MODEL)
    if n_ctx % tm == 0:
        r = jnp.where(row0 < n_ctx, n_b, b)
        return mods_ref[pl.ds(r, 1), cols]
    lat = mods_ref[pl.ds(b, 1), cols]
    ctx = mods_ref[pl.ds(n_b, 1), cols]
    rows = row0 + lax.broadcasted_iota(jnp.int32, (tm, 1), 0)
    return jnp.where(rows < n_ctx, ctx, lat)


def _modvec_kernel(c_ref, w_ref, b_ref, o_ref):
    c = c_ref[...]
    s = c * jax.nn.sigmoid(c)
    o_ref[0] = jnp.dot(s, w_ref[0], precision=HIGHEST, preferred_element_type=F32) + b_ref[0]


def _modvecs(cond, w_mod, b_mod):
    depth, d, n = w_mod.shape
    tn = _pick_tile(n, 1536, LANES)
    return pl.pallas_call(
        _modvec_kernel,
        grid=(depth, n // tn),
        in_specs=[pl.BlockSpec((8, d), lambda l, j: (0, 0)),
                  pl.BlockSpec((1, d, tn), lambda l, j: (l, 0, j)),
                  pl.BlockSpec((1, 1, tn), lambda l, j: (l, 0, j))],
        out_specs=pl.BlockSpec((1, 8, tn), lambda l, j: (l, 0, j)),
        out_shape=jax.ShapeDtypeStruct((depth, 8, n), F32),
        compiler_params=_cparams("parallel", "parallel"),
    )(cond, w_mod, b_mod.reshape(depth, 1, n))


def _inproj_kernel(x_ref, mods_ref, g_ref, w_ref, *o_refs, tm, n_ctx, n_b):
    b = pl.program_id(0)
    row0 = pl.program_id(1) * tm
    sh = _mod_vec(mods_ref, 0, b, row0, tm, n_ctx, n_b)
    sc = _mod_vec(mods_ref, 1, b, row0, tm, n_ctx, n_b)
    h = _norm_mod(x_ref[0], g_ref[...], sc, sh).astype(BF16)
    off = 0
    for o_ref, w in zip(o_refs, IN_WIDTHS):
        o_ref[0] = _dot(h, w_ref[:, off:off + w]).astype(o_ref.dtype)
        off += w


def _in_proj(xa, mods, g, w_cat, n_ctx):
    nb, l, d = xa.shape
    tm = _pick_tile(n_ctx, 256, 16)
    n = sum(IN_WIDTHS)
    kern = functools.partial(_inproj_kernel, tm=tm, n_ctx=n_ctx, n_b=nb)
    return pl.pallas_call(
        kern,
        grid=(nb, l // tm),
        in_specs=[pl.BlockSpec((1, tm, d), lambda b, i: (b, i, 0)),
                  pl.BlockSpec(mods.shape, lambda b, i: (0, 0)),
                  pl.BlockSpec((1, d), lambda b, i: (0, 0)),
                  pl.BlockSpec((d, n), lambda b, i: (0, 0))],
        out_specs=[pl.BlockSpec((1, tm, w), lambda b, i: (b, i, 0)) for w in IN_WIDTHS],
        out_shape=[jax.ShapeDtypeStruct((nb, l, w), F32 if k == 0 else BF16) for k, w in enumerate(IN_WIDTHS)],
        compiler_params=_cparams("parallel", "parallel"),
    )(xa, mods, g.reshape(1, d), w_cat)


def _prep_kernel(q_ref, k_ref, v_ref, cos_ref, sin_ref, gq_ref, gk_ref, qo_ref, ko_ref, vo_ref):
    cos = cos_ref[...]
    sin = sin_ref[...]
    lane = lax.broadcasted_iota(jnp.int32, (1, LANES), 1)
    first = (lane % (HEAD_DIM // 2)) < (HEAD_DIM // 4)
    low = lane < HEAD_DIM

    def rope(x):
        up = pltpu.roll(x, LANES - HEAD_DIM // 4, 1)
        dn = pltpu.roll(x, HEAD_DIM // 4, 1)
        return x * cos + jnp.where(first, up, dn) * sin

    gq = gq_ref[...] * (HEAD_DIM ** -0.5 * LOG2_E)
    for h in range(GQA_HEADS):
        x = q_ref[0, :, h * LANES:(h + 1) * LANES].astype(F32)
        ms = jnp.sum(x * x, axis=-1, keepdims=True) * (1.0 / HEAD_DIM)
        qo_ref[0, h] = rope(x * lax.rsqrt(ms + NORM_EPS) * gq).astype(qo_ref.dtype)
    x = k_ref[0].astype(F32)
    x2 = x * x
    s_lo = jnp.sum(jnp.where(low, x2, 0.0), axis=-1, keepdims=True)
    s_hi = jnp.sum(jnp.where(low, 0.0, x2), axis=-1, keepdims=True)
    ms = jnp.where(low, s_lo, s_hi) * (1.0 / HEAD_DIM)
    ko_ref[0] = rope(x * lax.rsqrt(ms + NORM_EPS) * gk_ref[...]).astype(ko_ref.dtype)
    v = v_ref[0]
    one = jnp.ones(v.shape, v.dtype)
    vo_ref[0, 0] = jnp.where(low, v, one)
    vo_ref[0, 1] = jnp.where(low, one, v)


def _prep_qkv(qa, ka, va, cos_t, sin_t, gq, gk):
    nb, l, _ = qa.shape
    tm = _pick_tile(l, 256, 16)
    tile2 = lambda g: jnp.concatenate([g, g]).reshape(1, LANES).astype(F32)
    tok = pl.BlockSpec((1, tm, LANES), lambda b, i: (b, i, 0))
    tab = pl.BlockSpec((tm, LANES), lambda b, i: (i, 0))
    vec = pl.BlockSpec((1, LANES), lambda b, i: (0, 0))
    return pl.pallas_call(
        _prep_kernel,
        grid=(nb, l // tm),
        in_specs=[pl.BlockSpec((1, tm, GQA_HEADS * LANES), lambda b, i: (b, i, 0)), tok, tok, tab, tab, vec, vec],
        out_specs=[pl.BlockSpec((1, GQA_HEADS, tm, LANES), lambda b, i: (b, 0, i, 0)), tok,
                   pl.BlockSpec((1, GQA_KV_HEADS, tm, LANES), lambda b, i: (b, 0, i, 0))],
        out_shape=[jax.ShapeDtypeStruct((nb, GQA_HEADS, l, LANES), BF16),
                   jax.ShapeDtypeStruct((nb, l, LANES), BF16),
                   jax.ShapeDtypeStruct((nb, GQA_KV_HEADS, l, LANES), BF16)],
        compiler_params=_cparams("parallel", "parallel"),
    )(qa, ka, va, cos_t, sin_t, tile2(gq), tile2(gk))


def _rope_tables(n_ctx, n_lat):
    t = jnp.arange(n_lat)
    pos = jnp.stack([(t // GRID_W).astype(F32), (t % GRID_W).astype(F32)], axis=1)
    n_freq = HEAD_DIM // 4
    inv = ROPE_BASE ** (-jnp.arange(n_freq, dtype=F32) / n_freq)
    lane = np.arange(LANES)
    axis = (lane % HEAD_DIM) // (HEAD_DIM // 2)
    freq = lane % n_freq
    sign = np.where((lane % (HEAD_DIM // 2)) < n_freq, -1.0, 1.0).astype(np.float32)
    ang = pos[:, axis] * inv[freq][None, :]
    cos_t = jnp.concatenate([jnp.ones((n_ctx, LANES), F32), jnp.cos(ang)], axis=0)
    sin_t = jnp.concatenate([jnp.zeros((n_ctx, LANES), F32), jnp.sin(ang) * sign[None, :]], axis=0)
    return cos_t, sin_t


def _gqa_kernel(q_ref, k_ref, v_ref, o_ref, s0_ref, s1_ref, p0_ref, p1_ref, acc_ref, *, tq, tk, n_ctx, n_chunks):
    i = pl.program_id(1)
    per_kv = GQA_HEADS // GQA_KV_HEADS
    half = per_kv * tq
    q = q_ref[0].reshape(GQA_HEADS * tq, LANES)
    s_slots, p_slots = (s0_ref, s1_ref), (p0_ref, p1_ref)

    def scores(rows, n, s_ref):
        s = _dot_nt(k_ref[0, rows, :], q)
        s_ref[0:n, :] = s
        return jnp.max(s, axis=0, keepdims=True)

    def softmax(n, s_ref, p_ref, m_prev, c_max):
        m_new = jnp.maximum(m_prev, c_max)
        p_ref[0:n, :] = jnp.exp2(s_ref[0:n, :] - m_new).astype(BF16)
        return m_new, jnp.exp2(m_prev - m_new)

    def weighted_values(rows, n, p_ref, alpha):
        for j in range(GQA_KV_HEADS):
            cols = slice(j * half, (j + 1) * half)
            upd = _dot_tn(v_ref[0, j, rows, :], p_ref[0:n, cols])
            acc_ref[:, cols] = upd if alpha is None else alpha[:, cols] * acc_ref[:, cols] + upd

    def finalize():
        pieces = []
        for h in range(GQA_HEADS):
            cols = slice(h * tq, (h + 1) * tq)
            j = h // per_kv
            den = acc_ref[(1 - j) * HEAD_DIM:(1 - j) * HEAD_DIM + 1, cols]
            pieces.append(acc_ref[j * HEAD_DIM:(j + 1) * HEAD_DIM, cols] * (1.0 / den))
        o_ref[0] = jnp.concatenate(pieces, axis=0).T.astype(o_ref.dtype)

    m_init = jnp.full((1, GQA_HEADS * tq), NEG_BIG, F32)

    @pl.when(i * tq < n_ctx)
    def _():
        rows = slice(0, n_ctx)
        c_max = scores(rows, n_ctx, s0_ref)
        softmax(n_ctx, s0_ref, p0_ref, m_init, c_max)
        weighted_values(rows, n_ctx, p0_ref, None)
        finalize()

    @pl.when(i * tq >= n_ctx)
    def _():
        chunk = lambda c: slice(c * tk, (c + 1) * tk)
        m, alpha, c_max = m_init, {}, {}
        c_max[0] = scores(chunk(0), tk, s_slots[0])
        for c in range(n_chunks):
            if c + 1 < n_chunks:
                c_max[c + 1] = scores(chunk(c + 1), tk, s_slots[(c + 1) % 2])
            m, alpha[c] = softmax(tk, s_slots[c % 2], p_slots[c % 2], m, c_max[c])
            if c >= 1:
                weighted_values(chunk(c - 1), tk, p_slots[(c - 1) % 2], alpha[c - 1] if c > 1 else None)
        last = n_chunks - 1
        weighted_values(chunk(last), tk, p_slots[last % 2], alpha[last] if last > 0 else None)
        finalize()


def _gqa_attention(qp, kp, vp, n_ctx):
    nb, _, l, _ = qp.shape
    tq = _pick_tile(n_ctx, 256, LANES)
    tk = _pick_tile(l, 768, LANES)
    assert n_ctx <= tk
    kern = functools.partial(_gqa_kernel, tq=tq, tk=tk, n_ctx=n_ctx, n_chunks=l // tk)
    return pl.pallas_call(
        kern,
        grid=(nb, l // tq),
        in_specs=[pl.BlockSpec((1, GQA_HEADS, tq, LANES), lambda b, i: (b, 0, i, 0)),
                  pl.BlockSpec((1, l, LANES), lambda b, i: (b, 0, 0)),
                  pl.BlockSpec((1, GQA_KV_HEADS, l, LANES), lambda b, i: (b, 0, 0, 0))],
        out_specs=pl.BlockSpec((1, tq, GQA_HEADS * HEAD_DIM), lambda b, i: (b, i, 0)),
        out_shape=jax.ShapeDtypeStruct((nb, l, GQA_HEADS * HEAD_DIM), BF16),
        scratch_shapes=[pltpu.VMEM((tk, GQA_HEADS * tq), F32), pltpu.VMEM((tk, GQA_HEADS * tq), F32),
                        pltpu.VMEM((tk, GQA_HEADS * tq), BF16), pltpu.VMEM((tk, GQA_HEADS * tq), BF16),
                        pltpu.VMEM((GQA_KV_WIDTH, GQA_HEADS * tq), F32)],
        compiler_params=_cparams("parallel", "arbitrary"),
    )(qp, kp, vp)


def _na_kernel(var_ref, q_ref, kc_ref, k0_ref, k1_ref, k2_ref, vc_ref, v0_ref, v1_ref, v2_ref, bm_ref, o_ref):
    del var_ref
    i = pl.program_id(1)
    lane = lax.broadcasted_iota(jnp.int32, (1, LANES), 1)

    def attend(pieces):
        def head_scores(h):
            pair, sub = divmod(h, 2)
            cols = slice(pair * LANES, (pair + 1) * LANES)
            own = (lane < HEAD_DIM) if sub == 0 else (lane >= HEAD_DIM)
            qh = (jnp.where(own, q_ref[0, :, cols], 0) * (HEAD_DIM ** -0.5)).astype(BF16)
            scores = []
            for k_ref, _, row0 in pieces:
                s = _dot_nt(k_ref[0, :, cols], qh)
                if row0 is not None:
                    s = s + bm_ref[0, h, row0:row0 + NA_BLOCK, :]
                scores.append(s)
            return scores

        def head_out(h, scores):
            pair, sub = divmod(h, 2)
            cols = slice(pair * LANES, (pair + 1) * LANES)
            m = functools.reduce(jnp.maximum, [jnp.max(s, axis=0, keepdims=True) for s in scores])
            den = 0.0
            acc = 0.0
            for s, (_, v_ref, _) in zip(scores, pieces):
                p = jnp.exp(s - m)
                den = den + jnp.sum(p, axis=0, keepdims=True)
                acc = acc + _dot_tn(v_ref[0, :, cols], p.astype(BF16))
            return acc[sub * HEAD_DIM:(sub + 1) * HEAD_DIM, :] * (1.0 / den)

        outs = []
        scores = head_scores(0)
        for h in range(NA_HEADS):
            nxt = head_scores(h + 1) if h + 1 < NA_HEADS else None
            outs.append(head_out(h, scores))
            scores = nxt
        o_ref[0] = jnp.concatenate(outs, axis=0).T.astype(o_ref.dtype)

    @pl.when(i == 0)
    def _():
        attend([(kc_ref, vc_ref, None)])

    @pl.when(i > 0)
    def _():
        attend([(k0_ref, v0_ref, 0), (k1_ref, v1_ref, NA_BLOCK), (k2_ref, v2_ref, 2 * NA_BLOCK),
                (kc_ref, vc_ref, None)])


def _na_bias_tables(rpb, n_lat):
    rows = n_lat // GRID_W
    n_blk = n_lat // NA_BLOCK
    slab_rows = NA_SLAB_BLOCKS * NA_BLOCK_ROWS
    assert rows >= slab_rows and rows >= NA_WIN_H
    sigs, var_of_blk = [], []
    for ib in range(n_blk):
        r0 = ib * NA_BLOCK_ROWS
        us = NA_BLOCK_ROWS * min(max(ib - 1, 0), n_blk - NA_SLAB_BLOCKS)
        rs = [min(max(r0 + g - NA_WIN_H // 2, 0), rows - NA_WIN_H) for g in range(NA_BLOCK_ROWS)]
        assert us <= min(rs) and max(rs) + NA_WIN_H <= us + slab_rows
        sig = (us - r0, tuple(r - r0 for r in rs))
        if sig not in sigs:
            sigs.append(sig)
        var_of_blk.append(sigs.index(sig))
    i_k = np.arange(slab_rows)[:, None, None, None]
    c_k = np.arange(GRID_W)[None, :, None, None]
    g_q = np.arange(NA_BLOCK_ROWS)[None, None, :, None]
    c_q = np.arange(GRID_W)[None, None, None, :]
    cs = np.clip(c_q - NA_WIN_W // 2, 0, GRID_W - NA_WIN_W)
    col_ok = (c_k >= cs) & (c_k < cs + NA_WIN_W)
    col_off = np.clip(c_k - c_q, -(NA_WIN_W - 1), NA_WIN_W - 1) + (NA_WIN_W - 1)
    n_co = 2 * NA_WIN_W - 1
    onehot = (col_off[0, :, 0, :][None] == np.arange(n_co)[:, None, None]).astype(np.float32)
    toe = jnp.einsum('hrc,ckq->hrkq', rpb.astype(F32), onehot, precision=HIGHEST)
    full = (slab_rows, GRID_W, NA_BLOCK_ROWS, GRID_W)
    tables = []
    for du, drs in sigs:
        drs = np.asarray(drs)[None, None, :, None]
        row_ok = (du + i_k >= drs) & (du + i_k < drs + NA_WIN_H)
        ok = np.broadcast_to(row_ok & col_ok, full).reshape(slab_rows * GRID_W, NA_BLOCK)
        row_off = np.clip(du + i_k[:, 0, :, 0] - g_q[0, 0, :, 0][None, :] + (NA_WIN_H - 1), 0, 2 * NA_WIN_H - 2)
        bias = toe[:, row_off].transpose(0, 1, 3, 2, 4)
        tables.append(jnp.where(ok[None], bias.reshape(NA_HEADS, slab_rows * GRID_W, NA_BLOCK), NEG_BIG))
    return jnp.stack(tables), jnp.asarray([0] + var_of_blk, jnp.int32)


def _na_attention(qn, kn, vn, bm, var_idx, n_ctx):
    nb, l, w = qn.shape
    assert n_ctx == NA_BLOCK
    n_lat_blk = (l - n_ctx) // NA_BLOCK
    blk = lambda f: pl.BlockSpec((1, NA_BLOCK, w), f)
    slab = lambda d: blk(lambda b, i, var: (b, 1 + jnp.clip(i - 2, 0, n_lat_blk - NA_SLAB_BLOCKS) + d, 0))
    ctx = blk(lambda b, i, var: (b, 0, 0))
    grid_spec = pltpu.PrefetchScalarGridSpec(
        num_scalar_prefetch=1,
        grid=(nb, l // NA_BLOCK),
        in_specs=[blk(lambda b, i, var: (b, i, 0)),
                  ctx, slab(0), slab(1), slab(2),
                  ctx, slab(0), slab(1), slab(2),
                  pl.BlockSpec((1,) + bm.shape[1:], lambda b, i, var: (var[i], 0, 0, 0))],
        out_specs=blk(lambda b, i, var: (b, i, 0)),
    )
    return pl.pallas_call(
        _na_kernel,
        grid_spec=grid_spec,
        out_shape=jax.ShapeDtypeStruct((nb, l, w), BF16),
        compiler_params=_cparams("parallel", "arbitrary"),
    )(var_idx, qn, kn, kn, kn, kn, vn, vn, vn, vn, bm)


def _swap_blocks(tiles, blk):
    n = len(tiles)
    out = [None] * n
    for d in range(n):
        mixed = tiles[d]
        for m in range(1, n):
            mixed = jnp.where(blk == m, tiles[(m + d) % n], mixed)
        if d:
            mixed = pltpu.roll(mixed, d * SSM_GROUP, 1)
        for m in range(n):
            out[m] = mixed if out[m] is None else jnp.where(blk == (m + d) % n, mixed, out[m])
    return out


def _s5_kernel(u_ref, t_ref, w_ref, vf_ref, vb_ref, a_ref, o_ref, ug_ref, xl_ref, xf_ref, xb_ref, y_ref, *, nc, ncc):
    blk = lax.broadcasted_iota(jnp.int32, (1, LANES), 1) // SSM_GROUP
    halves = S5_TILE // LANES
    for half in range(halves):
        rows = [u_ref[0, pl.ds(half * S5_GPT + sl, nc, stride=S5_CHUNK), :] for sl in range(S5_GPT)]
        for g, tile in enumerate(_swap_blocks(rows, blk)):
            ug_ref[g, :, half * LANES:(half + 1) * LANES] = tile.astype(BF16)
    for g in range(S5_GPT):
        xl_ref[g] = _dot(ug_ref[g], w_ref[g])
        y_ref[g] = _dot(ug_ref[g], t_ref[g])

    fwd = lax.broadcasted_iota(jnp.int32, (1, LANES), 1) < SSM_STATE
    re = slice(0, LANES)
    im = slice(LANES, 2 * LANES)
    sub = lax.broadcasted_iota(jnp.int32, (SUBLANES, 1), 0)
    nt, nct = nc // SUBLANES, ncc // SUBLANES

    def body(it, carry):
        tb = jnp.where(it < nct, nct - 1 - it, nt - 1 - (it - nct))
        rows_f = pl.ds(pl.multiple_of(it * SUBLANES, SUBLANES), SUBLANES)
        rows_b = pl.ds(pl.multiple_of(tb * SUBLANES, SUBLANES), SUBLANES)
        new = []
        for g in range(S5_GPT):
            r, m = carry[2 * g], carry[2 * g + 1]
            ar = a_ref[g, 0:1, :]
            ai = a_ref[g, 1:2, :]
            loc_f = xl_ref[g, rows_f, :]
            loc_b = xl_ref[g, rows_b, :]
            ent_f = ent_b = None
            for j in range(SUBLANES):
                jb = SUBLANES - 1 - j
                state = jnp.concatenate([r, m], axis=1)
                ent_f = state if j == 0 else jnp.where(sub == j, state, ent_f)
                ent_b = state if j == 0 else jnp.where(sub == jb, state, ent_b)
                lr = jnp.where(fwd, loc_f[j:j + 1, re], loc_b[jb:jb + 1, re])
                li = jnp.where(fwd, loc_f[j:j + 1, im], loc_b[jb:jb + 1, im])
                r, m = ar * r - ai * m + lr, ar * m + ai * r + li
            xf_ref[g, rows_f, :] = jnp.broadcast_to(ent_f, (SUBLANES, 2 * LANES))
            xb_ref[g, rows_b, :] = jnp.broadcast_to(ent_b, (SUBLANES, 2 * LANES))
            new += [r, m]
        return tuple(new)

    zero = jnp.zeros((1, LANES), F32)
    lax.fori_loop(0, nt, body, (zero,) * (2 * S5_GPT))
    for g in range(S5_GPT):
        y_ref[g] += _dot(xf_ref[g].astype(BF16), vf_ref[g]) + _dot(xb_ref[g].astype(BF16), vb_ref[g])
    for half in range(halves):
        tiles = [y_ref[g, :, half * LANES:(half + 1) * LANES] for g in range(S5_GPT)]
        for tl, tile in enumerate(_swap_blocks(tiles, blk)):
            o_ref[0, pl.ds(half * S5_GPT + tl, nc, stride=S5_CHUNK), :] = tile


def _s5_tables(lam_re, lam_im, log_dt, b_re, b_im, c_re, c_im, d_skip):
    hp = dict(precision=HIGHEST)
    dt = jnp.exp(log_dt.astype(F32))[:, :, None]
    lr, li = lam_re.astype(F32), lam_im.astype(F32)
    k = jnp.arange(S5_CHUNK + 1, dtype=F32)[:, None, None, None]
    mag = jnp.exp(k * (lr * dt)[None])
    pr = mag * jnp.cos(k * (li * dt)[None])
    pi = mag * jnp.sin(k * (li * dt)[None])
    den = lr * lr + li * li
    zr = ((pr[1] - 1.0) * lr + pi[1] * li) / den
    zi = (pi[1] * lr - (pr[1] - 1.0) * li) / den
    br, bi = b_re.astype(F32), b_im.astype(F32)
    bbr = zr[..., None] * br - zi[..., None] * bi
    bbi = zr[..., None] * bi + zi[..., None] * br
    cr, ci = c_re.astype(F32), c_im.astype(F32)
    clr = cr[None] * pr[:, :, :, None, :] - ci[None] * pi[:, :, :, None, :]
    cli = cr[None] * pi[:, :, :, None, :] + ci[None] * pr[:, :, :, None, :]
    kern = (jnp.einsum('kdgop,dgpi->kdgoi', clr, bbr, **hp)
            - jnp.einsum('kdgop,dgpi->kdgoi', cli, bbi, **hp))
    s_i = np.arange(S5_CHUNK)[:, None]
    t_i = np.arange(S5_CHUNK)[None, :]
    lag = t_i - s_i
    kf = jnp.where((lag >= 0)[:, :, None, None, None], kern[np.abs(lag), 0], 0.0)
    kb = jnp.where((lag <= 0)[:, :, None, None, None], kern[np.abs(lag), 1], 0.0)
    eye_h = jnp.eye(SSM_GROUP, dtype=F32)
    dsk = d_skip.astype(F32).reshape(SSM_GROUPS, SSM_GROUP)
    skip = (jnp.eye(S5_CHUNK, dtype=F32)[:, :, None, None, None]
            * (dsk[:, :, None] * eye_h[None])[None, None])
    tmat = (kf + kb + skip).transpose(2, 0, 4, 1, 3).reshape(SSM_GROUPS, S5_TILE, S5_TILE)
    pf_r, pf_i = pr[S5_CHUNK - 1 - np.arange(S5_CHUNK), 0], pi[S5_CHUNK - 1 - np.arange(S5_CHUNK), 0]
    pb_r, pb_i = pr[np.arange(S5_CHUNK), 1], pi[np.arange(S5_CHUNK), 1]

    def state_in(p_r, p_i, d):
        w_r = p_r[:, :, :, None] * bbr[d][None] - p_i[:, :, :, None] * bbi[d][None]
        w_i = p_r[:, :, :, None] * bbi[d][None] + p_i[:, :, :, None] * bbr[d][None]
        to_cols = lambda w: w.transpose(1, 0, 3, 2).reshape(SSM_GROUPS, S5_TILE, SSM_STATE)
        return to_cols(w_r), to_cols(w_i)

    wf_r, wf_i = state_in(pf_r, pf_i, 0)
    wb_r, wb_i = state_in(pb_r, pb_i, 1)
    wmat = jnp.concatenate([wf_r, wb_r, wf_i, wb_i], axis=-1)
    t_f = np.arange(S5_CHUNK) + 1
    t_b = S5_CHUNK - np.arange(S5_CHUNK)
    to_rows = lambda m: m.transpose(1, 3, 0, 2).reshape(SSM_GROUPS, SSM_STATE, S5_TILE)
    zeros = jnp.zeros((SSM_GROUPS, SSM_STATE, S5_TILE), F32)
    vf = jnp.concatenate([to_rows(clr[t_f, 0]), zeros, -to_rows(cli[t_f, 0]), zeros], axis=1)
    vb = jnp.concatenate([zeros, to_rows(clr[t_b, 1]), zeros, -to_rows(cli[t_b, 1])], axis=1)
    a_re = jnp.concatenate([pr[S5_CHUNK, 0], pr[S5_CHUNK, 1]], axis=-1)
    a_im = jnp.concatenate([pi[S5_CHUNK, 0], pi[S5_CHUNK, 1]], axis=-1)
    amat = jnp.concatenate([a_re[:, None], a_im[:, None], jnp.zeros((SSM_GROUPS, 6, LANES), F32)], axis=1)
    return tmat.astype(BF16), wmat.astype(BF16), vf.astype(BF16), vb.astype(BF16), amat


def _s5(u, tables, n_ctx):
    nb, l, w = u.shape
    nc = l // S5_CHUNK
    assert nc % SUBLANES == 0 and (n_ctx // S5_CHUNK) % SUBLANES == 0 and S5_GPT * SSM_GROUP == LANES
    tmat, wmat, vf, vb, amat = tables
    mat = pl.BlockSpec((S5_GPT, S5_TILE, S5_TILE), lambda j, b: (j, 0, 0))
    tok = pl.BlockSpec((1, l, LANES), lambda j, b: (b, 0, j))
    kern = functools.partial(_s5_kernel, nc=nc, ncc=n_ctx // S5_CHUNK)
    return pl.pallas_call(
        kern,
        grid=(w // LANES, nb),
        in_specs=[tok, mat, mat, mat, mat, pl.BlockSpec((S5_GPT, 8, LANES), lambda j, b: (j, 0, 0))],
        out_specs=tok,
        out_shape=jax.ShapeDtypeStruct((nb, l, w), F32),
        scratch_shapes=[pltpu.VMEM((S5_GPT, nc, S5_TILE), BF16)] + [pltpu.VMEM((S5_GPT, nc, S5_TILE), F32)] * 4,
        compiler_params=_cparams("parallel", "parallel"),
    )(u, tmat, wmat, vf, vb, amat)


def _merge_kernel(y_ref, a_ref, n_ref, gt_ref, x_ref, mods_ref, wglu_ref, wbr_ref, wout_ref, o_ref,
                  *, tm, n_ctx, n_b):
    b = pl.program_id(0)
    row0 = pl.program_id(1) * tm
    z = jax.nn.gelu(y_ref[0].astype(F32))
    glu = z * jax.nn.sigmoid(_dot(z.astype(BF16), wglu_ref[...]))
    gate = lambda k: jax.nn.sigmoid(gt_ref[0, :, k * D_MODEL:(k + 1) * D_MODEL].astype(F32))
    merged = gate(0) * _dot(glu.astype(BF16), wbr_ref[0])
    merged = merged + gate(1) * _dot(a_ref[0], wbr_ref[1])
    merged = merged + gate(2) * _dot(n_ref[0], wbr_ref[2])
    y = _dot(merged.astype(BF16), wout_ref[...])
    g1 = _mod_vec(mods_ref, 2, b, row0, tm, n_ctx, n_b)
    o_ref[0] = x_ref[0] + g1 * y


def _merge(y_ssm, gqa, na, gates, xa, mods, w_glu, w_branch, w_out, n_ctx):
    nb, l, d = xa.shape
    tm = _pick_tile(n_ctx, 256, 16)
    tok = lambda w: pl.BlockSpec((1, tm, w), lambda b, i: (b, i, 0))
    full = lambda a: pl.BlockSpec(a.shape, lambda b, i: (0,) * a.ndim)
    kern = functools.partial(_merge_kernel, tm=tm, n_ctx=n_ctx, n_b=nb)
    return pl.pallas_call(
        kern,
        grid=(nb, l // tm),
        in_specs=[tok(SSM_WIDTH), tok(SSM_WIDTH), tok(NA_WIDTH), tok(N_BRANCH * d), tok(d),
                  full(mods), full(w_glu), full(w_branch), full(w_out)],
        out_specs=tok(d),
        out_shape=jax.ShapeDtypeStruct((nb, l, d), F32),
        compiler_params=_cparams("parallel", "parallel"),
    )(y_ssm, gqa, na, gates, xa, mods, w_glu, w_branch, w_out)


def _top2_sum(v):
    hi1, lo1 = jnp.maximum(v[0], v[1]), jnp.minimum(v[0], v[1])
    hi2, lo2 = jnp.maximum(v[2], v[3]), jnp.minimum(v[2], v[3])
    return jnp.maximum(hi1, hi2) + jnp.maximum(jnp.minimum(hi1, hi2), jnp.maximum(lo1, lo2))


def _first_argmax(vals):
    best, idx = vals[0], jnp.zeros(vals[0].shape, jnp.int32)
    for j in range(1, len(vals)):
        upd = vals[j] > best
        best = jnp.where(upd, vals[j], best)
        idx = jnp.where(upd, j, idx)
    return best, idx


def _router_kernel(x_ref, mods_ref, g_ref, rw_ref, rb_ref, o_ref, *, tm, n_ctx, n_b):
    b = pl.program_id(0)
    row0 = pl.program_id(1) * tm
    sh = _mod_vec(mods_ref, 3, b, row0, tm, n_ctx, n_b)
    sc = _mod_vec(mods_ref, 4, b, row0, tm, n_ctx, n_b)
    h = _norm_mod(x_ref[0], g_ref[...], sc, sh)
    logits = lax.dot_general(rw_ref[...], h, (((1,), (1,)), ((), ())), precision=HIGHEST,
                             preferred_element_type=F32)
    rows = [logits[e:e + 1, :] for e in range(N_EXPERTS)]
    mx = functools.reduce(jnp.maximum, rows)
    ex = [jnp.exp(r - mx) for r in rows]
    den = functools.reduce(lambda a, c: a + c, ex)
    probs = [e / den for e in ex]
    sel = [probs[e] + rb_ref[e:e + 1, :] for e in range(N_EXPERTS)]
    grp = [sel[g * EXPERTS_PER_GROUP:(g + 1) * EXPERTS_PER_GROUP] for g in range(N_EXPERT_GROUPS)]
    _, best = _first_argmax([_top2_sum(v) for v in grp])
    pick = lambda table, j: functools.reduce(
        lambda a, g: jnp.where(best == g, table[g * EXPERTS_PER_GROUP + j], a),
        range(1, N_EXPERT_GROUPS), table[j])
    sel_g = [pick(sel, j) for j in range(EXPERTS_PER_GROUP)]
    prob_g = [pick(probs, j) for j in range(EXPERTS_PER_GROUP)]
    _, i1 = _first_argmax(sel_g)
    _, i2 = _first_argmax([jnp.where(i1 == j, -jnp.inf, sel_g[j]) for j in range(EXPERTS_PER_GROUP)])
    at = lambda idx: functools.reduce(lambda a, j: jnp.where(idx == j, prob_g[j], a),
                                      range(1, EXPERTS_PER_GROUP), prob_g[0])
    w1, w2 = at(i1), at(i2)
    tot = w1 + w2
    w1, w2 = w1 / tot, w2 / tot
    out = []
    for e in range(N_EXPERTS):
        g, j = divmod(e, EXPERTS_PER_GROUP)
        c = jnp.where(i1 == j, w1, 0.0) + jnp.where(i2 == j, w2, 0.0)
        out.append(jnp.where(best == g, c, 0.0))
    o_ref[0] = jnp.concatenate(out, axis=0)


def _router(xa, mods, g, router_w, router_b, n_ctx):
    nb, l, d = xa.shape
    tm = _pick_tile(n_ctx, 256, LANES)
    kern = functools.partial(_router_kernel, tm=tm, n_ctx=n_ctx, n_b=nb)
    comb_t = pl.pallas_call(
        kern,
        grid=(nb, l // tm),
        in_specs=[pl.BlockSpec((1, tm, d), lambda b, i: (b, i, 0)),
                  pl.BlockSpec(mods.shape, lambda b, i: (0, 0)),
                  pl.BlockSpec((1, d), lambda b, i: (0, 0)),
                  pl.BlockSpec((N_EXPERTS, d), lambda b, i: (0, 0)),
                  pl.BlockSpec((N_EXPERTS, 1), lambda b, i: (0, 0))],
        out_specs=pl.BlockSpec((1, N_EXPERTS, tm), lambda b, i: (b, 0, i)),
        out_shape=jax.ShapeDtypeStruct((nb, N_EXPERTS, l), F32),
        compiler_params=_cparams("parallel", "parallel"),
    )(xa, mods, g.reshape(1, d), router_w.T.astype(F32), router_b.reshape(N_EXPERTS, 1).astype(F32))
    return comb_t


def _moe_kernel(cnt_ref, x_ref, combt_ref, mods_ref, g_ref, tri_ref, wg_ref, wu_ref, wd_ref, o_ref,
                h_ref, acc_ref, rankt_ref, *, tm, n_ctx, n_b):
    b = pl.program_id(0)
    i = pl.program_id(1)
    ep = pl.program_id(2)
    row0 = i * tm

    @pl.when(ep == 0)
    def _():
        sh = _mod_vec(mods_ref, 3, b, row0, tm, n_ctx, n_b)
        sc = _mod_vec(mods_ref, 4, b, row0, tm, n_ctx, n_b)
        h_ref[...] = _norm_mod(x_ref[0], g_ref[...], sc, sh).astype(BF16)
        acc_ref[...] = jnp.zeros(acc_ref.shape, F32)
        routed_t = jnp.where(combt_ref[0] != 0.0, 1.0, 0.0).astype(BF16)
        rankt_ref[...] = _dot_nt(routed_t, tri_ref[...])

    base = ((b * pl.num_programs(1) + i) * MOE_PAIRS + ep) * MOE_PAIR
    n_slots = jnp.maximum(cnt_ref[base], cnt_ref[base + 1])
    slot_r = lax.broadcasted_iota(jnp.int32, (MOE_CAP, 1), 0).astype(F32)
    rows = [(rankt_ref[pl.ds(ep * MOE_PAIR + k, 1), :], combt_ref[0, pl.ds(ep * MOE_PAIR + k, 1), :])
            for k in range(MOE_PAIR)]

    def body(j, carry):
        first = (j * MOE_CAP).astype(F32)
        sel = [(rank == first + slot_r) & (w != 0.0) for rank, w in rows]
        one_hot = jnp.concatenate([jnp.where(m, 1.0, 0.0).astype(BF16) for m in sel], axis=0)
        xc = _dot(one_hot, h_ref[...]).astype(BF16)
        ys = []
        for k in range(MOE_PAIR):
            xk = xc[k * MOE_CAP:(k + 1) * MOE_CAP]
            gate = _dot(xk, wg_ref[k])
            act = (gate * jax.nn.sigmoid(gate)) * _dot(xk, wu_ref[k])
            y = _dot(act.astype(BF16), wd_ref[k])
            weight = jnp.sum(jnp.where(sel[k], rows[k][1], 0.0), axis=1, keepdims=True)
            ys.append((y * weight).astype(BF16))
        acc_ref[...] += _dot_tn(one_hot, jnp.concatenate(ys, axis=0))
        return carry

    lax.fori_loop(0, (n_slots + MOE_CAP - 1) // MOE_CAP, body, 0)

    @pl.when(ep == MOE_PAIRS - 1)
    def _():
        g2 = _mod_vec(mods_ref, 5, b, row0, tm, n_ctx, n_b)
        o_ref[0] = x_ref[0] + g2 * acc_ref[...]


def _moe(xa, comb_t, mods, g, w_gate, w_up, w_down, n_ctx):
    nb, l, d = xa.shape
    tm = _pick_tile(l, 768, LANES)
    nt = l // tm
    counts = (comb_t != 0.0).reshape(nb, N_EXPERTS, nt, tm).sum(axis=-1).astype(jnp.int32)
    counts = counts.transpose(0, 2, 1).reshape(-1)
    tri = jnp.asarray(np.tril(np.ones((tm, tm), np.float32), -1), BF16)
    kern = functools.partial(_moe_kernel, tm=tm, n_ctx=n_ctx, n_b=nb)
    grid_spec = pltpu.PrefetchScalarGridSpec(
        num_scalar_prefetch=1,
        grid=(nb, nt, MOE_PAIRS),
        in_specs=[pl.BlockSpec((1, tm, d), lambda b, i, e, cnt: (b, i, 0)),
                  pl.BlockSpec((1, N_EXPERTS, tm), lambda b, i, e, cnt: (b, 0, i)),
                  pl.BlockSpec(mods.shape, lambda b, i, e, cnt: (0, 0)),
                  pl.BlockSpec((1, d), lambda b, i, e, cnt: (0, 0)),
                  pl.BlockSpec((tm, tm), lambda b, i, e, cnt: (0, 0)),
                  pl.BlockSpec((MOE_PAIR, d, EXPERT_FF), lambda b, i, e, cnt: (e, 0, 0)),
                  pl.BlockSpec((MOE_PAIR, d, EXPERT_FF), lambda b, i, e, cnt: (e, 0, 0)),
                  pl.BlockSpec((MOE_PAIR, EXPERT_FF, d), lambda b, i, e, cnt: (e, 0, 0))],
        out_specs=pl.BlockSpec((1, tm, d), lambda b, i, e, cnt: (b, i, 0)),
        scratch_shapes=[pltpu.VMEM((tm, d), BF16), pltpu.VMEM((tm, d), F32), pltpu.VMEM((N_EXPERTS, tm), F32)],
    )
    return pl.pallas_call(
        kern,
        grid_spec=grid_spec,
        out_shape=jax.ShapeDtypeStruct((nb, l, d), F32),
        compiler_params=_cparams("parallel", "parallel", "arbitrary"),
    )(counts, xa, comb_t, mods, g.reshape(1, d), tri, w_gate, w_up, w_down)


def _final_norm_kernel(x_ref, g_ref, o_ref):
    x = x_ref[0]
    ms = jnp.mean(x * x, axis=-1, keepdims=True)
    o_ref[0] = x * lax.rsqrt(ms + NORM_EPS) * g_ref[...]


def _final_norm(xa, g, n_ctx):
    nb, l, d = xa.shape
    tm = _pick_tile(n_ctx, 256, 8)
    off = n_ctx // tm
    return pl.pallas_call(
        _final_norm_kernel,
        grid=(nb, (l - n_ctx) // tm),
        in_specs=[pl.BlockSpec((1, tm, d), lambda b, i: (b, i + off, 0)),
                  pl.BlockSpec((1, d), lambda b, i: (0, 0))],
        out_specs=pl.BlockSpec((1, tm, d), lambda b, i: (b, i, 0)),
        out_shape=jax.ShapeDtypeStruct((nb, l - n_ctx, d), F32),
        compiler_params=_cparams("parallel", "parallel"),
    )(xa, g.reshape(1, d).astype(F32))


def _in_proj_weights(w_in):
    w_u, w_qa, w_rest = jnp.split(w_in, [SSM_WIDTH, SSM_WIDTH + GQA_HEADS * HEAD_DIM], axis=-1)
    d = w_in.shape[0]
    per_kv = GQA_HEADS // GQA_KV_HEADS
    tiles = []
    for h in range(GQA_HEADS):
        w_h = w_qa[:, h * HEAD_DIM:(h + 1) * HEAD_DIM]
        zero = jnp.zeros((d, HEAD_DIM), w_in.dtype)
        tiles += [w_h, zero] if h // per_kv == 0 else [zero, w_h]
    return jnp.concatenate([w_u] + tiles + [w_rest], axis=-1).astype(BF16)


def kernel(x, c, ctx, c_ctx, w_mod, b_mod, norm1_g, norm2_g, w_in, ssm_lam_re, ssm_lam_im, ssm_log_dt, ssm_b_re, ssm_b_im, ssm_c_re, ssm_c_im, ssm_d, ssm_w_glu, gqa_q_norm_g, gqa_k_norm_g, na_rpb, w_branch, w_out, router_w, router_b, moe_w_gate, moe_w_up, moe_w_down, final_norm_g):
    nb, n_lat, d = x.shape
    n_ctx = ctx.shape[1]
    depth = w_mod.shape[0]
    assert d == D_MODEL and nb < 8 and GQA_KV_HEADS == 2 and sum(IN_WIDTHS) % LANES == 0
    assert n_ctx % S5_CHUNK == 0 and n_lat % NA_BLOCK == 0

    xa = jnp.concatenate([ctx, x], axis=1).astype(F32)
    cond = jnp.zeros((8, d), F32).at[:nb].set(c.astype(F32)).at[nb].set(c_ctx.astype(F32))
    mods = _modvecs(cond, w_mod.astype(F32), b_mod.astype(F32))
    cos_t, sin_t = _rope_tables(n_ctx, n_lat)

    for layer in range(depth):
        u, qa, ka, va, qn, kn, vn, gates = _in_proj(xa, mods[layer], norm1_g[layer].astype(F32),
                                                    _in_proj_weights(w_in[layer]), n_ctx)
        qp, kp, vp = _prep_qkv(qa, ka, va, cos_t, sin_t, gqa_q_norm_g[layer], gqa_k_norm_g[layer])
        tables = _s5_tables(ssm_lam_re[layer], ssm_lam_im[layer], ssm_log_dt[layer], ssm_b_re[layer],
                            ssm_b_im[layer], ssm_c_re[layer], ssm_c_im[layer], ssm_d[layer])
        y_ssm = _s5(u, tables, n_ctx)
        gqa = _gqa_attention(qp, kp, vp, n_ctx)
        bm, var_idx = _na_bias_tables(na_rpb[layer], n_lat)
        na = _na_attention(qn, kn, vn, bm, var_idx, n_ctx)
        xa = _merge(y_ssm, gqa, na, gates, xa, mods[layer], ssm_w_glu[layer].astype(BF16),
                    w_branch[layer].astype(BF16), w_out[layer].astype(BF16), n_ctx)
        comb = _router(xa, mods[layer], norm2_g[layer].astype(F32), router_w, router_b, n_ctx)
        xa = _moe(xa, comb, mods[layer], norm2_g[layer].astype(F32), moe_w_gate[layer].astype(BF16),
                  moe_w_up[layer].astype(BF16), moe_w_down[layer].astype(BF16), n_ctx)
    return _final_norm(xa, final_norm_g, n_ctx)
```

```python
import functools
import math

import numpy as np
import jax
import jax.numpy as jnp
from jax import lax
from jax.experimental import pallas as pl
from jax.experimental.pallas import tpu as pltpu

D_MODEL = 1024
GRID_W = 64
HEAD_DIM = 64
SSM_WIDTH = 512
SSM_GROUP = 16
SSM_GROUPS = SSM_WIDTH // SSM_GROUP
SSM_STATE = 64
GQA_HEADS = 8
GQA_KV_HEADS = 2
GQA_KV_WIDTH = GQA_KV_HEADS * HEAD_DIM
NA_HEADS = 8
NA_WIDTH = NA_HEADS * HEAD_DIM
NA_WIN_H = 8
NA_WIN_W = 16
ROPE_BASE = 10000.0
N_BRANCH = 3
N_EXPERTS = 16
N_EXPERT_GROUPS = 4
EXPERTS_PER_GROUP = N_EXPERTS // N_EXPERT_GROUPS
EXPERT_FF = 512
NORM_EPS = 1e-6

LANES = 128
SUBLANES = 8
S5_CHUNK = 16
S5_TILE = S5_CHUNK * SSM_GROUP
S5_GPT = 8
NA_BLOCK_ROWS = 4
NA_BLOCK = NA_BLOCK_ROWS * GRID_W
MOE_PAIR = 2
MOE_PAIRS = N_EXPERTS // MOE_PAIR
MOE_CAP = 128
NA_SLAB_BLOCKS = 3
NEG_BIG = -1e30
LOG2_E = math.log2(math.e)
VMEM_LIMIT = 56 * 1024 * 1024

F32 = jnp.float32
BF16 = jnp.bfloat16
HIGHEST = lax.Precision.HIGHEST

IN_WIDTHS = (SSM_WIDTH, GQA_HEADS * LANES, GQA_KV_WIDTH, GQA_KV_WIDTH, NA_WIDTH, NA_WIDTH, NA_WIDTH,
             N_BRANCH * D_MODEL)


def _cparams(*sem):
    return pltpu.CompilerParams(dimension_semantics=sem, vmem_limit_bytes=VMEM_LIMIT)


def _pick_tile(n, cap, mult):
    best = None
    for t in range(mult, min(n, cap) + 1, mult):
        if n % t == 0:
            best = t
    assert best is not None, (n, cap, mult)
    return best


def _dot(a, b):
    return jnp.dot(a, b, preferred_element_type=F32)


def _dot_nt(a, b):
    return lax.dot_general(a, b, (((1,), (1,)), ((), ())), preferred_element_type=F32)


def _dot_tn(a, b):
    return lax.dot_general(a, b, (((0,), (0,)), ((), ())), preferred_element_type=F32)


def _norm_mod(x, g, sc, sh):
    ms = jnp.mean(x * x, axis=-1, keepdims=True)
    return (x * lax.rsqrt(ms + NORM_EPS) * g) * (1.0 + sc) + sh


def _mod_vec(mods_ref, seg, b, row0, tm, n_ctx, n_b):
    cols = slice(seg * D_MODEL, (seg + 1) * D_MODEL)
    if n_ctx % tm == 0:
        r = jnp.where(row0 < n_ctx, n_b, b)
        return mods_ref[pl.ds(r, 1), cols]
    lat = mods_ref[pl.ds(b, 1), cols]
    ctx = mods_ref[pl.ds(n_b, 1), cols]
    rows = row0 + lax.broadcasted_iota(jnp.int32, (tm, 1), 0)
    return jnp.where(rows < n_ctx, ctx, lat)


def _modvec_kernel(c_ref, w_ref, b_ref, o_ref):
    c = c_ref[...]
    s = c * jax.nn.sigmoid(c)
    o_ref[0] = jnp.dot(s, w_ref[0], precision=HIGHEST, preferred_element_type=F32) + b_ref[0]


def _modvecs(cond, w_mod, b_mod):
    depth, d, n = w_mod.shape
    tn = _pick_tile(n, 1536, LANES)
    return pl.pallas_call(
        _modvec_kernel,
        grid=(depth, n // tn),
        in_specs=[pl.BlockSpec((8, d), lambda l, j: (0, 0)),
                  pl.BlockSpec((1, d, tn), lambda l, j: (l, 0, j)),
                  pl.BlockSpec((1, 1, tn), lambda l, j: (l, 0, j))],
        out_specs=pl.BlockSpec((1, 8, tn), lambda l, j: (l, 0, j)),
        out_shape=jax.ShapeDtypeStruct((depth, 8, n), F32),
        compiler_params=_cparams("parallel", "parallel"),
    )(cond, w_mod, b_mod.reshape(depth, 1, n))


def _inproj_kernel(x_ref, mods_ref, g_ref, w_ref, cos_ref, sin_ref, gq_ref, gk_ref,
                   u_ref, qp_ref, kp_ref, vp_ref, qn_ref, kn_ref, vn_ref, gt_ref, *, tm, n_ctx, n_b):
    b = pl.program_id(0)
    row0 = pl.program_id(1) * tm
    sh = _mod_vec(mods_ref, 0, b, row0, tm, n_ctx, n_b)
    sc = _mod_vec(mods_ref, 1, b, row0, tm, n_ctx, n_b)
    h = _norm_mod(x_ref[0], g_ref[...], sc, sh).astype(BF16)
    offs = np.cumsum((0,) + IN_WIDTHS)
    proj = lambda k: _dot(h, w_ref[:, offs[k]:offs[k + 1]])
    u_ref[0] = proj(0)
    for k, o_ref in ((4, qn_ref), (5, kn_ref), (6, vn_ref), (7, gt_ref)):
        o_ref[0] = proj(k).astype(o_ref.dtype)

    cos = cos_ref[...]
    sin = sin_ref[...]
    lane = lax.broadcasted_iota(jnp.int32, (1, LANES), 1)
    first = (lane % (HEAD_DIM // 2)) < (HEAD_DIM // 4)
    low = lane < HEAD_DIM

    def rope(x):
        up = pltpu.roll(x, LANES - HEAD_DIM // 4, 1)
        dn = pltpu.roll(x, HEAD_DIM // 4, 1)
        return x * cos + jnp.where(first, up, dn) * sin

    gq = gq_ref[...] * (HEAD_DIM ** -0.5 * LOG2_E)
    q = proj(1)
    for hd in range(GQA_HEADS):
        x = q[:, hd * LANES:(hd + 1) * LANES]
        ms = jnp.sum(x * x, axis=-1, keepdims=True) * (1.0 / HEAD_DIM)
        qp_ref[0, hd] = rope(x * lax.rsqrt(ms + NORM_EPS) * gq).astype(qp_ref.dtype)
    x = proj(2)
    x2 = x * x
    s_lo = jnp.sum(jnp.where(low, x2, 0.0), axis=-1, keepdims=True)
    s_hi = jnp.sum(jnp.where(low, 0.0, x2), axis=-1, keepdims=True)
    ms = jnp.where(low, s_lo, s_hi) * (1.0 / HEAD_DIM)
    kp_ref[0] = rope(x * lax.rsqrt(ms + NORM_EPS) * gk_ref[...]).astype(kp_ref.dtype)
    v = proj(3)
    vp_ref[0, 0] = jnp.where(low, v, 1.0).astype(vp_ref.dtype)
    vp_ref[0, 1] = jnp.where(low, 1.0, v).astype(vp_ref.dtype)


def _in_proj(xa, mods, g, w_cat, cos_t, sin_t, gq, gk, n_ctx):
    nb, l, d = xa.shape
    tm = _pick_tile(n_ctx, 256, 16)
    n = sum(IN_WIDTHS)
    tile2 = lambda v: jnp.concatenate([v, v]).reshape(1, LANES).astype(F32)
    tok = lambda w: pl.BlockSpec((1, tm, w), lambda b, i: (b, i, 0))
    heads = lambda k: pl.BlockSpec((1, k, tm, LANES), lambda b, i: (b, 0, i, 0))
    tab = pl.BlockSpec((tm, LANES), lambda b, i: (i, 0))
    vec = lambda w: pl.BlockSpec((1, w), lambda b, i: (0, 0))
    act = lambda w, dt=BF16: jax.ShapeDtypeStruct((nb, l, w), dt)
    kern = functools.partial(_inproj_kernel, tm=tm, n_ctx=n_ctx, n_b=nb)
    return pl.pallas_call(
        kern,
        grid=(nb, l // tm),
        in_specs=[tok(d), pl.BlockSpec(mods.shape, lambda b, i: (0, 0)), vec(d),
                  pl.BlockSpec((d, n), lambda b, i: (0, 0)), tab, tab, vec(LANES), vec(LANES)],
        out_specs=[tok(SSM_WIDTH), heads(GQA_HEADS), tok(LANES), heads(GQA_KV_HEADS),
                   tok(NA_WIDTH), tok(NA_WIDTH), tok(NA_WIDTH), tok(N_BRANCH * d)],
        out_shape=[act(SSM_WIDTH, F32), jax.ShapeDtypeStruct((nb, GQA_HEADS, l, LANES), BF16), act(LANES),
                   jax.ShapeDtypeStruct((nb, GQA_KV_HEADS, l, LANES), BF16),
                   act(NA_WIDTH), act(NA_WIDTH), act(NA_WIDTH), act(N_BRANCH * d)],
        compiler_params=_cparams("parallel", "parallel"),
    )(xa, mods, g.reshape(1, d), w_cat, cos_t, sin_t, tile2(gq), tile2(gk))


def _rope_tables(n_ctx, n_lat):
    t = jnp.arange(n_lat)
    pos = jnp.stack([(t // GRID_W).astype(F32), (t % GRID_W).astype(F32)], axis=1)
    n_freq = HEAD_DIM // 4
    inv = ROPE_BASE ** (-jnp.arange(n_freq, dtype=F32) / n_freq)
    lane = np.arange(LANES)
    axis = (lane % HEAD_DIM) // (HEAD_DIM // 2)
    freq = lane % n_freq
    sign = np.where((lane % (HEAD_DIM // 2)) < n_freq, -1.0, 1.0).astype(np.float32)
    ang = pos[:, axis] * inv[freq][None, :]
    cos_t = jnp.concatenate([jnp.ones((n_ctx, LANES), F32), jnp.cos(ang)], axis=0)
    sin_t = jnp.concatenate([jnp.zeros((n_ctx, LANES), F32), jnp.sin(ang) * sign[None, :]], axis=0)
    return cos_t, sin_t


def _gqa_kernel(q_ref, k_ref, v_ref, o_ref, s0_ref, s1_ref, p0_ref, p1_ref, acc_ref, *, tq, tk, n_ctx, n_chunks):
    i = pl.program_id(1)
    per_kv = GQA_HEADS // GQA_KV_HEADS
    half = per_kv * tq
    q = q_ref[0].reshape(GQA_HEADS * tq, LANES)
    s_slots, p_slots = (s0_ref, s1_ref), (p0_ref, p1_ref)

    def scores(rows, n, s_ref):
        s = _dot_nt(k_ref[0, rows, :], q)
        s_ref[0:n, :] = s
        return jnp.max(s, axis=0, keepdims=True)

    def softmax(n, s_ref, p_ref, m_prev, c_max):
        m_new = jnp.maximum(m_prev, c_max)
        p_ref[0:n, :] = jnp.exp2(s_ref[0:n, :] - m_new).astype(BF16)
        return m_new, jnp.exp2(m_prev - m_new)

    def weighted_values(rows, n, p_ref, alpha):
        for j in range(GQA_KV_HEADS):
            cols = slice(j * half, (j + 1) * half)
            upd = _dot_tn(v_ref[0, j, rows, :], p_ref[0:n, cols])
            acc_ref[:, cols] = upd if alpha is None else alpha[:, cols] * acc_ref[:, cols] + upd

    def finalize():
        pieces = []
        for h in range(GQA_HEADS):
            cols = slice(h * tq, (h + 1) * tq)
            j = h // per_kv
            den = acc_ref[(1 - j) * HEAD_DIM:(1 - j) * HEAD_DIM + 1, cols]
            pieces.append(acc_ref[j * HEAD_DIM:(j + 1) * HEAD_DIM, cols] * (1.0 / den))
        o_ref[0] = jnp.concatenate(pieces, axis=0).T.astype(o_ref.dtype)

    m_init = jnp.full((1, GQA_HEADS * tq), NEG_BIG, F32)

    @pl.when(i * tq < n_ctx)
    def _():
        rows = slice(0, n_ctx)
        c_max = scores(rows, n_ctx, s0_ref)
        softmax(n_ctx, s0_ref, p0_ref, m_init, c_max)
        weighted_values(rows, n_ctx, p0_ref, None)
        finalize()

    @pl.when(i * tq >= n_ctx)
    def _():
        chunk = lambda c: slice(c * tk, (c + 1) * tk)
        m, alpha, c_max = m_init, {}, {}
        c_max[0] = scores(chunk(0), tk, s_slots[0])
        for c in range(n_chunks):
            if c + 1 < n_chunks:
                c_max[c + 1] = scores(chunk(c + 1), tk, s_slots[(c + 1) % 2])
            m, alpha[c] = softmax(tk, s_slots[c % 2], p_slots[c % 2], m, c_max[c])
            if c >= 1:
                weighted_values(chunk(c - 1), tk, p_slots[(c - 1) % 2], alpha[c - 1] if c > 1 else None)
        last = n_chunks - 1
        weighted_values(chunk(last), tk, p_slots[last % 2], alpha[last] if last > 0 else None)
        finalize()


def _gqa_attention(qp, kp, vp, n_ctx):
    nb, _, l, _ = qp.shape
    tq = _pick_tile(n_ctx, 256, LANES)
    tk = _pick_tile(l, 768, LANES)
    assert n_ctx <= tk
    kern = functools.partial(_gqa_kernel, tq=tq, tk=tk, n_ctx=n_ctx, n_chunks=l // tk)
    return pl.pallas_call(
        kern,
        grid=(nb, l // tq),
        in_specs=[pl.BlockSpec((1, GQA_HEADS, tq, LANES), lambda b, i: (b, 0, i, 0)),
                  pl.BlockSpec((1, l, LANES), lambda b, i: (b, 0, 0)),
                  pl.BlockSpec((1, GQA_KV_HEADS, l, LANES), lambda b, i: (b, 0, 0, 0))],
        out_specs=pl.BlockSpec((1, tq, GQA_HEADS * HEAD_DIM), lambda b, i: (b, i, 0)),
        out_shape=jax.ShapeDtypeStruct((nb, l, GQA_HEADS * HEAD_DIM), BF16),
        scratch_shapes=[pltpu.VMEM((tk, GQA_HEADS * tq), F32), pltpu.VMEM((tk, GQA_HEADS * tq), F32),
                        pltpu.VMEM((tk, GQA_HEADS * tq), BF16), pltpu.VMEM((tk, GQA_HEADS * tq), BF16),
                        pltpu.VMEM((GQA_KV_WIDTH, GQA_HEADS * tq), F32)],
        compiler_params=_cparams("parallel", "arbitrary"),
    )(qp, kp, vp)


def _na_kernel(var_ref, q_ref, kc_ref, k0_ref, k1_ref, k2_ref, vc_ref, v0_ref, v1_ref, v2_ref, bm_ref, o_ref):
    del var_ref
    i = pl.program_id(1)
    lane = lax.broadcasted_iota(jnp.int32, (1, LANES), 1)

    def attend(pieces):
        def head_scores(h):
            pair, sub = divmod(h, 2)
            cols = slice(pair * LANES, (pair + 1) * LANES)
            own = (lane < HEAD_DIM) if sub == 0 else (lane >= HEAD_DIM)
            qh = (jnp.where(own, q_ref[0, :, cols], 0) * (HEAD_DIM ** -0.5)).astype(BF16)
            scores = []
            for k_ref, _, row0 in pieces:
                s = _dot_nt(k_ref[0, :, cols], qh)
                if row0 is not None:
                    s = s + bm_ref[0, h, row0:row0 + NA_BLOCK, :]
                scores.append(s)
            return scores

        def head_out(h, scores):
            pair, sub = divmod(h, 2)
            cols = slice(pair * LANES, (pair + 1) * LANES)
            m = functools.reduce(jnp.maximum, [jnp.max(s, axis=0, keepdims=True) for s in scores])
            den = 0.0
            acc = 0.0
            for s, (_, v_ref, _) in zip(scores, pieces):
                p = jnp.exp(s - m)
                den = den + jnp.sum(p, axis=0, keepdims=True)
                acc = acc + _dot_tn(v_ref[0, :, cols], p.astype(BF16))
            return acc[sub * HEAD_DIM:(sub + 1) * HEAD_DIM, :] * (1.0 / den)

        outs = []
        scores = head_scores(0)
        for h in range(NA_HEADS):
            nxt = head_scores(h + 1) if h + 1 < NA_HEADS else None
            outs.append(head_out(h, scores))
            scores = nxt
        o_ref[0] = jnp.concatenate(outs, axis=0).T.astype(o_ref.dtype)

    @pl.when(i == 0)
    def _():
        attend([(kc_ref, vc_ref, None)])

    @pl.when(i > 0)
    def _():
        attend([(k0_ref, v0_ref, 0), (k1_ref, v1_ref, NA_BLOCK), (k2_ref, v2_ref, 2 * NA_BLOCK),
                (kc_ref, vc_ref, None)])


def _na_bias_tables(rpb, n_lat):
    rows = n_lat // GRID_W
    n_blk = n_lat // NA_BLOCK
    slab_rows = NA_SLAB_BLOCKS * NA_BLOCK_ROWS
    assert rows >= slab_rows and rows >= NA_WIN_H
    sigs, var_of_blk = [], []
    for ib in range(n_blk):
        r0 = ib * NA_BLOCK_ROWS
        us = NA_BLOCK_ROWS * min(max(ib - 1, 0), n_blk - NA_SLAB_BLOCKS)
        rs = [min(max(r0 + g - NA_WIN_H // 2, 0), rows - NA_WIN_H) for g in range(NA_BLOCK_ROWS)]
        assert us <= min(rs) and max(rs) + NA_WIN_H <= us + slab_rows
        sig = (us - r0, tuple(r - r0 for r in rs))
        if sig not in sigs:
            sigs.append(sig)
        var_of_blk.append(sigs.index(sig))
    i_k = np.arange(slab_rows)[:, None, None, None]
    c_k = np.arange(GRID_W)[None, :, None, None]
    g_q = np.arange(NA_BLOCK_ROWS)[None, None, :, None]
    c_q = np.arange(GRID_W)[None, None, None, :]
    cs = np.clip(c_q - NA_WIN_W // 2, 0, GRID_W - NA_WIN_W)
    col_ok = (c_k >= cs) & (c_k < cs + NA_WIN_W)
    col_off = np.clip(c_k - c_q, -(NA_WIN_W - 1), NA_WIN_W - 1) + (NA_WIN_W - 1)
    n_co = 2 * NA_WIN_W - 1
    onehot = (col_off[0, :, 0, :][None] == np.arange(n_co)[:, None, None]).astype(np.float32)
    toe = jnp.einsum('hrc,ckq->hrkq', rpb.astype(F32), onehot, precision=HIGHEST)
    full = (slab_rows, GRID_W, NA_BLOCK_ROWS, GRID_W)
    tables = []
    for du, drs in sigs:
        drs = np.asarray(drs)[None, None, :, None]
        row_ok = (du + i_k >= drs) & (du + i_k < drs + NA_WIN_H)
        ok = np.broadcast_to(row_ok & col_ok, full).reshape(slab_rows * GRID_W, NA_BLOCK)
        row_off = np.clip(du + i_k[:, 0, :, 0] - g_q[0, 0, :, 0][None, :] + (NA_WIN_H - 1), 0, 2 * NA_WIN_H - 2)
        bias = toe[:, row_off].transpose(0, 1, 3, 2, 4)
        tables.append(jnp.where(ok[None], bias.reshape(rpb.shape[0], slab_rows * GRID_W, NA_BLOCK), NEG_BIG))
    return jnp.stack(tables), jnp.asarray([0] + var_of_blk, jnp.int32)


def _na_attention(qn, kn, vn, bm, var_idx, n_ctx):
    nb, l, w = qn.shape
    assert n_ctx == NA_BLOCK
    n_lat_blk = (l - n_ctx) // NA_BLOCK
    blk = lambda f: pl.BlockSpec((1, NA_BLOCK, w), f)
    slab = lambda d: blk(lambda b, i, var: (b, 1 + jnp.clip(i - 2, 0, n_lat_blk - NA_SLAB_BLOCKS) + d, 0))
    ctx = blk(lambda b, i, var: (b, 0, 0))
    grid_spec = pltpu.PrefetchScalarGridSpec(
        num_scalar_prefetch=1,
        grid=(nb, l // NA_BLOCK),
        in_specs=[blk(lambda b, i, var: (b, i, 0)),
                  ctx, slab(0), slab(1), slab(2),
                  ctx, slab(0), slab(1), slab(2),
                  pl.BlockSpec((1,) + bm.shape[1:], lambda b, i, var: (var[i], 0, 0, 0))],
        out_specs=blk(lambda b, i, var: (b, i, 0)),
    )
    return pl.pallas_call(
        _na_kernel,
        grid_spec=grid_spec,
        out_shape=jax.ShapeDtypeStruct((nb, l, w), BF16),
        compiler_params=_cparams("parallel", "arbitrary"),
    )(var_idx, qn, kn, kn, kn, kn, vn, vn, vn, vn, bm)


def _swap_blocks(tiles, blk):
    n = len(tiles)
    out = [None] * n
    for d in range(n):
        mixed = tiles[d]
        for m in range(1, n):
            mixed = jnp.where(blk == m, tiles[(m + d) % n], mixed)
        if d:
            mixed = pltpu.roll(mixed, d * SSM_GROUP, 1)
        for m in range(n):
            out[m] = mixed if out[m] is None else jnp.where(blk == (m + d) % n, mixed, out[m])
    return out


def _s5_kernel(u_ref, t_ref, w_ref, vf_ref, vb_ref, a_ref, o_ref, ug_ref, xl_ref, xf_ref, xb_ref, y_ref, *, nc, ncc):
    blk = lax.broadcasted_iota(jnp.int32, (1, LANES), 1) // SSM_GROUP
    halves = S5_TILE // LANES
    for half in range(halves):
        rows = [u_ref[0, pl.ds(half * S5_GPT + sl, nc, stride=S5_CHUNK), :] for sl in range(S5_GPT)]
        for g, tile in enumerate(_swap_blocks(rows, blk)):
            ug_ref[g, :, half * LANES:(half + 1) * LANES] = tile.astype(BF16)
    for g in range(S5_GPT):
        xl_ref[g] = _dot(ug_ref[g], w_ref[g])
        y_ref[g] = _dot(ug_ref[g], t_ref[g])

    fwd = lax.broadcasted_iota(jnp.int32, (1, LANES), 1) < SSM_STATE
    re = slice(0, LANES)
    im = slice(LANES, 2 * LANES)
    sub = lax.broadcasted_iota(jnp.int32, (SUBLANES, 1), 0)
    nt, nct = nc // SUBLANES, ncc // SUBLANES

    def body(it, carry):
        tb = jnp.where(it < nct, nct - 1 - it, nt - 1 - (it - nct))
        rows_f = pl.ds(pl.multiple_of(it * SUBLANES, SUBLANES), SUBLANES)
        rows_b = pl.ds(pl.multiple_of(tb * SUBLANES, SUBLANES), SUBLANES)
        new = []
        for g in range(S5_GPT):
            r, m = carry[2 * g], carry[2 * g + 1]
            ar = a_ref[g, 0:1, :]
            ai = a_ref[g, 1:2, :]
            loc_f = xl_ref[g, rows_f, :]
            loc_b = xl_ref[g, rows_b, :]
            ent_f = ent_b = None
            for j in range(SUBLANES):
                jb = SUBLANES - 1 - j
                state = jnp.concatenate([r, m], axis=1)
                ent_f = state if j == 0 else jnp.where(sub == j, state, ent_f)
                ent_b = state if j == 0 else jnp.where(sub == jb, state, ent_b)
                lr = jnp.where(fwd, loc_f[j:j + 1, re], loc_b[jb:jb + 1, re])
                li = jnp.where(fwd, loc_f[j:j + 1, im], loc_b[jb:jb + 1, im])
                r, m = ar * r - ai * m + lr, ar * m + ai * r + li
            xf_ref[g, rows_f, :] = jnp.broadcast_to(ent_f, (SUBLANES, 2 * LANES))
            xb_ref[g, rows_b, :] = jnp.broadcast_to(ent_b, (SUBLANES, 2 * LANES))
            new += [r, m]
        return tuple(new)

    zero = jnp.zeros((1, LANES), F32)
    lax.fori_loop(0, nt, body, (zero,) * (2 * S5_GPT))
    for g in range(S5_GPT):
        y_ref[g] += _dot(xf_ref[g].astype(BF16), vf_ref[g]) + _dot(xb_ref[g].astype(BF16), vb_ref[g])
    for half in range(halves):
        tiles = [y_ref[g, :, half * LANES:(half + 1) * LANES] for g in range(S5_GPT)]
        for tl, tile in enumerate(_swap_blocks(tiles, blk)):
            o_ref[0, pl.ds(half * S5_GPT + tl, nc, stride=S5_CHUNK), :] = tile


def _s5_tables(lam_re, lam_im, log_dt, b_re, b_im, c_re, c_im, d_skip):
    hp = dict(precision=HIGHEST)
    dt = jnp.exp(log_dt.astype(F32))[:, :, None]
    lr, li = lam_re.astype(F32), lam_im.astype(F32)
    k = jnp.arange(S5_CHUNK + 1, dtype=F32)[:, None, None, None]
    mag = jnp.exp(k * (lr * dt)[None])
    pr = mag * jnp.cos(k * (li * dt)[None])
    pi = mag * jnp.sin(k * (li * dt)[None])
    den = lr * lr + li * li
    zr = ((pr[1] - 1.0) * lr + pi[1] * li) / den
    zi = (pi[1] * lr - (pr[1] - 1.0) * li) / den
    br, bi = b_re.astype(F32), b_im.astype(F32)
    bbr = zr[..., None] * br - zi[..., None] * bi
    bbi = zr[..., None] * bi + zi[..., None] * br
    cr, ci = c_re.astype(F32), c_im.astype(F32)
    clr = cr[None] * pr[:, :, :, None, :] - ci[None] * pi[:, :, :, None, :]
    cli = cr[None] * pi[:, :, :, None, :] + ci[None] * pr[:, :, :, None, :]
    kern = (jnp.einsum('kdgop,dgpi->kdgoi', clr, bbr, **hp)
            - jnp.einsum('kdgop,dgpi->kdgoi', cli, bbi, **hp))
    s_i = np.arange(S5_CHUNK)[:, None]
    t_i = np.arange(S5_CHUNK)[None, :]
    lag = t_i - s_i
    kf = jnp.where((lag >= 0)[:, :, None, None, None], kern[np.abs(lag), 0], 0.0)
    kb = jnp.where((lag <= 0)[:, :, None, None, None], kern[np.abs(lag), 1], 0.0)
    eye_h = jnp.eye(SSM_GROUP, dtype=F32)
    dsk = d_skip.astype(F32).reshape(SSM_GROUPS, SSM_GROUP)
    skip = (jnp.eye(S5_CHUNK, dtype=F32)[:, :, None, None, None]
            * (dsk[:, :, None] * eye_h[None])[None, None])
    tmat = (kf + kb + skip).transpose(2, 0, 4, 1, 3).reshape(SSM_GROUPS, S5_TILE, S5_TILE)
    pf_r, pf_i = pr[S5_CHUNK - 1 - np.arange(S5_CHUNK), 0], pi[S5_CHUNK - 1 - np.arange(S5_CHUNK), 0]
    pb_r, pb_i = pr[np.arange(S5_CHUNK), 1], pi[np.arange(S5_CHUNK), 1]

    def state_in(p_r, p_i, d):
        w_r = p_r[:, :, :, None] * bbr[d][None] - p_i[:, :, :, None] * bbi[d][None]
        w_i = p_r[:, :, :, None] * bbi[d][None] + p_i[:, :, :, None] * bbr[d][None]
        to_cols = lambda w: w.transpose(1, 0, 3, 2).reshape(SSM_GROUPS, S5_TILE, SSM_STATE)
        return to_cols(w_r), to_cols(w_i)

    wf_r, wf_i = state_in(pf_r, pf_i, 0)
    wb_r, wb_i = state_in(pb_r, pb_i, 1)
    wmat = jnp.concatenate([wf_r, wb_r, wf_i, wb_i], axis=-1)
    t_f = np.arange(S5_CHUNK) + 1
    t_b = S5_CHUNK - np.arange(S5_CHUNK)
    to_rows = lambda m: m.transpose(1, 3, 0, 2).reshape(SSM_GROUPS, SSM_STATE, S5_TILE)
    zeros = jnp.zeros((SSM_GROUPS, SSM_STATE, S5_TILE), F32)
    vf = jnp.concatenate([to_rows(clr[t_f, 0]), zeros, -to_rows(cli[t_f, 0]), zeros], axis=1)
    vb = jnp.concatenate([zeros, to_rows(clr[t_b, 1]), zeros, -to_rows(cli[t_b, 1])], axis=1)
    a_re = jnp.concatenate([pr[S5_CHUNK, 0], pr[S5_CHUNK, 1]], axis=-1)
    a_im = jnp.concatenate([pi[S5_CHUNK, 0], pi[S5_CHUNK, 1]], axis=-1)
    amat = jnp.concatenate([a_re[:, None], a_im[:, None], jnp.zeros((SSM_GROUPS, 6, LANES), F32)], axis=1)
    return tmat.astype(BF16), wmat.astype(BF16), vf.astype(BF16), vb.astype(BF16), amat


def _s5(u, tables, n_ctx):
    nb, l, w = u.shape
    nc = l // S5_CHUNK
    assert nc % SUBLANES == 0 and (n_ctx // S5_CHUNK) % SUBLANES == 0 and S5_GPT * SSM_GROUP == LANES
    tmat, wmat, vf, vb, amat = tables
    mat = pl.BlockSpec((S5_GPT, S5_TILE, S5_TILE), lambda j, b: (j, 0, 0))
    tok = pl.BlockSpec((1, l, LANES), lambda j, b: (b, 0, j))
    kern = functools.partial(_s5_kernel, nc=nc, ncc=n_ctx // S5_CHUNK)
    return pl.pallas_call(
        kern,
        grid=(w // LANES, nb),
        in_specs=[tok, mat, mat, mat, mat, pl.BlockSpec((S5_GPT, 8, LANES), lambda j, b: (j, 0, 0))],
        out_specs=tok,
        out_shape=jax.ShapeDtypeStruct((nb, l, w), F32),
        scratch_shapes=[pltpu.VMEM((S5_GPT, nc, S5_TILE), BF16)] + [pltpu.VMEM((S5_GPT, nc, S5_TILE), F32)] * 4,
        compiler_params=_cparams("parallel", "parallel"),
    )(u, tmat, wmat, vf, vb, amat)


def _merge_kernel(y_ref, a_ref, n_ref, gt_ref, x_ref, mods_ref, wglu_ref, wbr_ref, wout_ref, g2_ref,
                  rw_ref, rb_ref, o_ref, comb_ref, *, tm, n_ctx, n_b):
    b = pl.program_id(0)
    row0 = pl.program_id(1) * tm
    z = jax.nn.gelu(y_ref[0].astype(F32))
    glu = z * jax.nn.sigmoid(_dot(z.astype(BF16), wglu_ref[...]))
    gate = lambda k: jax.nn.sigmoid(gt_ref[0, :, k * D_MODEL:(k + 1) * D_MODEL].astype(F32))
    merged = gate(0) * _dot(glu.astype(BF16), wbr_ref[0])
    merged = merged + gate(1) * _dot(a_ref[0], wbr_ref[1])
    merged = merged + gate(2) * _dot(n_ref[0], wbr_ref[2])
    y = _dot(merged.astype(BF16), wout_ref[...])
    g1 = _mod_vec(mods_ref, 2, b, row0, tm, n_ctx, n_b)
    x_new = x_ref[0] + g1 * y
    o_ref[0] = x_new
    sh2 = _mod_vec(mods_ref, 3, b, row0, tm, n_ctx, n_b)
    sc2 = _mod_vec(mods_ref, 4, b, row0, tm, n_ctx, n_b)
    comb_ref[0] = _route(_norm_mod(x_new, g2_ref[...], sc2, sh2), rw_ref, rb_ref)


def _merge(y_ssm, gqa, na, gates, xa, mods, w_glu, w_branch, w_out, g2, router, n_ctx):
    nb, l, d = xa.shape
    tm = _pick_tile(n_ctx, 256, LANES)
    tok = lambda w: pl.BlockSpec((1, tm, w), lambda b, i: (b, i, 0))
    full = lambda a: pl.BlockSpec(a.shape, lambda b, i: (0,) * a.ndim)
    consts = (mods, w_glu, w_branch, w_out, g2.reshape(1, d)) + tuple(router)
    kern = functools.partial(_merge_kernel, tm=tm, n_ctx=n_ctx, n_b=nb)
    return pl.pallas_call(
        kern,
        grid=(nb, l // tm),
        in_specs=[tok(SSM_WIDTH), tok(SSM_WIDTH), tok(NA_WIDTH), tok(N_BRANCH * d), tok(d)]
                 + [full(a) for a in consts],
        out_specs=[tok(d), pl.BlockSpec((1, N_EXPERTS, tm), lambda b, i: (b, 0, i))],
        out_shape=[jax.ShapeDtypeStruct((nb, l, d), F32), jax.ShapeDtypeStruct((nb, N_EXPERTS, l), F32)],
        compiler_params=_cparams("parallel", "parallel"),
    )(y_ssm, gqa, na, gates, xa, *consts)


def _top2_sum(v):
    hi1, lo1 = jnp.maximum(v[0], v[1]), jnp.minimum(v[0], v[1])
    hi2, lo2 = jnp.maximum(v[2], v[3]), jnp.minimum(v[2], v[3])
    return jnp.maximum(hi1, hi2) + jnp.maximum(jnp.minimum(hi1, hi2), jnp.maximum(lo1, lo2))


def _first_argmax(vals):
    best, idx = vals[0], jnp.zeros(vals[0].shape, jnp.int32)
    for j in range(1, len(vals)):
        upd = vals[j] > best
        best = jnp.where(upd, vals[j], best)
        idx = jnp.where(upd, j, idx)
    return best, idx


def _route(h, rw_ref, rb_ref):
    logits = lax.dot_general(rw_ref[...], h, (((1,), (1,)), ((), ())), precision=HIGHEST,
                             preferred_element_type=F32)
    rows = [logits[e:e + 1, :] for e in range(N_EXPERTS)]
    mx = functools.reduce(jnp.maximum, rows)
    ex = [jnp.exp(r - mx) for r in rows]
    den = functools.reduce(lambda a, c: a + c, ex)
    probs = [e / den for e in ex]
    sel = [probs[e] + rb_ref[e:e + 1, :] for e in range(N_EXPERTS)]
    grp = [sel[g * EXPERTS_PER_GROUP:(g + 1) * EXPERTS_PER_GROUP] for g in range(N_EXPERT_GROUPS)]
    _, best = _first_argmax([_top2_sum(v) for v in grp])
    pick = lambda table, j: functools.reduce(
        lambda a, g: jnp.where(best == g, table[g * EXPERTS_PER_GROUP + j], a),
        range(1, N_EXPERT_GROUPS), table[j])
    sel_g = [pick(sel, j) for j in range(EXPERTS_PER_GROUP)]
    prob_g = [pick(probs, j) for j in range(EXPERTS_PER_GROUP)]
    _, i1 = _first_argmax(sel_g)
    _, i2 = _first_argmax([jnp.where(i1 == j, -jnp.inf, sel_g[j]) for j in range(EXPERTS_PER_GROUP)])
    at = lambda idx: functools.reduce(lambda a, j: jnp.where(idx == j, prob_g[j], a),
                                      range(1, EXPERTS_PER_GROUP), prob_g[0])
    w1, w2 = at(i1), at(i2)
    tot = w1 + w2
    w1, w2 = w1 / tot, w2 / tot
    out = []
    for e in range(N_EXPERTS):
        g, j = divmod(e, EXPERTS_PER_GROUP)
        c = jnp.where(i1 == j, w1, 0.0) + jnp.where(i2 == j, w2, 0.0)
        out.append(jnp.where(best == g, c, 0.0))
    return jnp.concatenate(out, axis=0)


def _moe_kernel(cnt_ref, x_ref, combt_ref, mods_ref, g_ref, tri_ref, wg_ref, wu_ref, wd_ref, o_ref,
                h_ref, acc_ref, rankt_ref, *, tm, n_ctx, n_b):
    b = pl.program_id(0)
    i = pl.program_id(1)
    ep = pl.program_id(2)
    row0 = i * tm

    @pl.when(ep == 0)
    def _():
        sh = _mod_vec(mods_ref, 3, b, row0, tm, n_ctx, n_b)
        sc = _mod_vec(mods_ref, 4, b, row0, tm, n_ctx, n_b)
        h_ref[...] = _norm_mod(x_ref[0], g_ref[...], sc, sh).astype(BF16)
        acc_ref[...] = jnp.zeros(acc_ref.shape, F32)
        routed_t = jnp.where(combt_ref[0] != 0.0, 1.0, 0.0).astype(BF16)
        rankt_ref[...] = _dot_nt(routed_t, tri_ref[...])

    base = ((b * pl.num_programs(1) + i) * MOE_PAIRS + ep) * MOE_PAIR
    n_slots = jnp.maximum(cnt_ref[base], cnt_ref[base + 1])
    slot_r = lax.broadcasted_iota(jnp.int32, (MOE_CAP, 1), 0).astype(F32)
    rows = [(rankt_ref[pl.ds(ep * MOE_PAIR + k, 1), :], combt_ref[0, pl.ds(ep * MOE_PAIR + k, 1), :])
            for k in range(MOE_PAIR)]

    def body(j, carry):
        first = (j * MOE_CAP).astype(F32)
        sel = [(rank == first + slot_r) & (w != 0.0) for rank, w in rows]
        one_hot = jnp.concatenate([jnp.where(m, 1.0, 0.0).astype(BF16) for m in sel], axis=0)
        xc = _dot(one_hot, h_ref[...]).astype(BF16)
        ys = []
        for k in range(MOE_PAIR):
            xk = xc[k * MOE_CAP:(k + 1) * MOE_CAP]
            gate = _dot(xk, wg_ref[k])
            act = (gate * jax.nn.sigmoid(gate)) * _dot(xk, wu_ref[k])
            y = _dot(act.astype(BF16), wd_ref[k])
            weight = jnp.sum(jnp.where(sel[k], rows[k][1], 0.0), axis=1, keepdims=True)
            ys.append((y * weight).astype(BF16))
        acc_ref[...] += _dot_tn(one_hot, jnp.concatenate(ys, axis=0))
        return carry

    lax.fori_loop(0, (n_slots + MOE_CAP - 1) // MOE_CAP, body, 0)

    @pl.when(ep == MOE_PAIRS - 1)
    def _():
        g2 = _mod_vec(mods_ref, 5, b, row0, tm, n_ctx, n_b)
        o_ref[0] = x_ref[0] + g2 * acc_ref[...]


def _moe(xa, comb_t, mods, g, w_gate, w_up, w_down, n_ctx):
    nb, l, d = xa.shape
    tm = _pick_tile(l, 768, LANES)
    nt = l // tm
    counts = (comb_t != 0.0).reshape(nb, N_EXPERTS, nt, tm).sum(axis=-1).astype(jnp.int32)
    counts = counts.transpose(0, 2, 1).reshape(-1)
    tri = jnp.asarray(np.tril(np.ones((tm, tm), np.float32), -1), BF16)
    kern = functools.partial(_moe_kernel, tm=tm, n_ctx=n_ctx, n_b=nb)
    grid_spec = pltpu.PrefetchScalarGridSpec(
        num_scalar_prefetch=1,
        grid=(nb, nt, MOE_PAIRS),
        in_specs=[pl.BlockSpec((1, tm, d), lambda b, i, e, cnt: (b, i, 0)),
                  pl.BlockSpec((1, N_EXPERTS, tm), lambda b, i, e, cnt: (b, 0, i)),
                  pl.BlockSpec(mods.shape, lambda b, i, e, cnt: (0, 0)),
                  pl.BlockSpec((1, d), lambda b, i, e, cnt: (0, 0)),
                  pl.BlockSpec((tm, tm), lambda b, i, e, cnt: (0, 0)),
                  pl.BlockSpec((MOE_PAIR, d, EXPERT_FF), lambda b, i, e, cnt: (e, 0, 0)),
                  pl.BlockSpec((MOE_PAIR, d, EXPERT_FF), lambda b, i, e, cnt: (e, 0, 0)),
                  pl.BlockSpec((MOE_PAIR, EXPERT_FF, d), lambda b, i, e, cnt: (e, 0, 0))],
        out_specs=pl.BlockSpec((1, tm, d), lambda b, i, e, cnt: (b, i, 0)),
        scratch_shapes=[pltpu.VMEM((tm, d), BF16), pltpu.VMEM((tm, d), F32), pltpu.VMEM((N_EXPERTS, tm), F32)],
    )
    return pl.pallas_call(
        kern,
        grid_spec=grid_spec,
        out_shape=jax.ShapeDtypeStruct((nb, l, d), F32),
        compiler_params=_cparams("parallel", "parallel", "arbitrary"),
    )(counts, xa, comb_t, mods, g.reshape(1, d), tri, w_gate, w_up, w_down)


def _final_norm_kernel(x_ref, g_ref, o_ref):
    x = x_ref[0]
    ms = jnp.mean(x * x, axis=-1, keepdims=True)
    o_ref[0] = x * lax.rsqrt(ms + NORM_EPS) * g_ref[...]


def _final_norm(xa, g, n_ctx):
    nb, l, d = xa.shape
    tm = _pick_tile(n_ctx, 256, 8)
    off = n_ctx // tm
    return pl.pallas_call(
        _final_norm_kernel,
        grid=(nb, (l - n_ctx) // tm),
        in_specs=[pl.BlockSpec((1, tm, d), lambda b, i: (b, i + off, 0)),
                  pl.BlockSpec((1, d), lambda b, i: (0, 0))],
        out_specs=pl.BlockSpec((1, tm, d), lambda b, i: (b, i, 0)),
        out_shape=jax.ShapeDtypeStruct((nb, l - n_ctx, d), F32),
        compiler_params=_cparams("parallel", "parallel"),
    )(xa, g.reshape(1, d).astype(F32))


def _in_proj_weights(w_in):
    w_u, w_qa, w_rest = jnp.split(w_in, [SSM_WIDTH, SSM_WIDTH + GQA_HEADS * HEAD_DIM], axis=-1)
    per_kv = GQA_HEADS // GQA_KV_HEADS
    zero = jnp.zeros(w_in.shape[:-1] + (HEAD_DIM,), w_in.dtype)
    tiles = []
    for h in range(GQA_HEADS):
        w_h = w_qa[..., h * HEAD_DIM:(h + 1) * HEAD_DIM]
        tiles += [w_h, zero] if h // per_kv == 0 else [zero, w_h]
    return jnp.concatenate([w_u] + tiles + [w_rest], axis=-1).astype(BF16)


def kernel(x, c, ctx, c_ctx, w_mod, b_mod, norm1_g, norm2_g, w_in, ssm_lam_re, ssm_lam_im, ssm_log_dt, ssm_b_re, ssm_b_im, ssm_c_re, ssm_c_im, ssm_d, ssm_w_glu, gqa_q_norm_g, gqa_k_norm_g, na_rpb, w_branch, w_out, router_w, router_b, moe_w_gate, moe_w_up, moe_w_down, final_norm_g):
    nb, n_lat, d = x.shape
    n_ctx = ctx.shape[1]
    depth = w_mod.shape[0]
    assert d == D_MODEL and nb < 8 and GQA_KV_HEADS == 2 and sum(IN_WIDTHS) % LANES == 0
    assert n_ctx % S5_CHUNK == 0 and n_lat % NA_BLOCK == 0

    xa = jnp.concatenate([ctx, x], axis=1).astype(F32)
    cond = jnp.zeros((8, d), F32).at[:nb].set(c.astype(F32)).at[nb].set(c_ctx.astype(F32))
    mods = _modvecs(cond, w_mod.astype(F32), b_mod.astype(F32))
    cos_t, sin_t = _rope_tables(n_ctx, n_lat)
    w_cat = _in_proj_weights(w_in)
    s5_tables = jax.vmap(_s5_tables)(ssm_lam_re, ssm_lam_im, ssm_log_dt, ssm_b_re, ssm_b_im, ssm_c_re, ssm_c_im,
                                     ssm_d)
    bm, var_idx = _na_bias_tables(na_rpb.reshape((depth * NA_HEADS,) + na_rpb.shape[2:]), n_lat)
    bm = bm.reshape((bm.shape[0], depth, NA_HEADS) + bm.shape[2:])
    w_glu, w_br, w_o = ssm_w_glu.astype(BF16), w_branch.astype(BF16), w_out.astype(BF16)
    w_g, w_u, w_d = moe_w_gate.astype(BF16), moe_w_up.astype(BF16), moe_w_down.astype(BF16)
    n1, n2 = norm1_g.astype(F32), norm2_g.astype(F32)
    router = (router_w.T.astype(F32), router_b.reshape(N_EXPERTS, 1).astype(F32))

    for layer in range(depth):
        u, qp, kp, vp, qn, kn, vn, gates = _in_proj(xa, mods[layer], n1[layer], w_cat[layer], cos_t, sin_t,
                                                    gqa_q_norm_g[layer], gqa_k_norm_g[layer], n_ctx)
        y_ssm = _s5(u, [t[layer] for t in s5_tables], n_ctx)
        gqa = _gqa_attention(qp, kp, vp, n_ctx)
        na = _na_attention(qn, kn, vn, bm[:, layer], var_idx, n_ctx)
        xa, comb_t = _merge(y_ssm, gqa, na, gates, xa, mods[layer], w_glu[layer], w_br[layer], w_o[layer],
                            n2[layer], router, n_ctx)
        xa = _moe(xa, comb_t, mods[layer], n2[layer], w_g[layer], w_u[layer], w_d[layer], n_ctx)
    return _final_norm(xa, final_norm_g, n_ctx)
```

```python
import functools
import math

import numpy as np
import jax
import jax.numpy as jnp
from jax import lax
from jax.experimental import pallas as pl
from jax.experimental.pallas import tpu as pltpu

D_MODEL = 1024
GRID_W = 64
HEAD_DIM = 64
SSM_WIDTH = 512
SSM_GROUP = 16
SSM_GROUPS = SSM_WIDTH // SSM_GROUP
SSM_STATE = 64
GQA_HEADS = 8
GQA_KV_HEADS = 2
GQA_KV_WIDTH = GQA_KV_HEADS * HEAD_DIM
NA_HEADS = 8
NA_WIDTH = NA_HEADS * HEAD_DIM
NA_WIN_H = 8
NA_WIN_W = 16
ROPE_BASE = 10000.0
N_BRANCH = 3
N_EXPERTS = 16
N_EXPERT_GROUPS = 4
EXPERTS_PER_GROUP = N_EXPERTS // N_EXPERT_GROUPS
EXPERT_FF = 512
NORM_EPS = 1e-6

LANES = 128
SUBLANES = 8
S5_CHUNK = 16
S5_TILE = S5_CHUNK * SSM_GROUP
S5_GPT = 8
NA_BLOCK_ROWS = 4
NA_BLOCK = NA_BLOCK_ROWS * GRID_W
MOE_PAIR = 4
MOE_PAIRS = N_EXPERTS // MOE_PAIR
MOE_CAP = 128
NA_SLAB_BLOCKS = 3
NEG_BIG = -1e30
LOG2_E = math.log2(math.e)
VMEM_LIMIT = 56 * 1024 * 1024

F32 = jnp.float32
BF16 = jnp.bfloat16
HIGHEST = lax.Precision.HIGHEST

IN_WIDTHS = (SSM_WIDTH, GQA_HEADS * LANES, GQA_KV_WIDTH, GQA_KV_WIDTH, NA_WIDTH, NA_WIDTH, NA_WIDTH,
             N_BRANCH * D_MODEL)


def _cparams(*sem):
    return pltpu.CompilerParams(dimension_semantics=sem, vmem_limit_bytes=VMEM_LIMIT)


def _pick_tile(n, cap, mult):
    best = None
    for t in range(mult, min(n, cap) + 1, mult):
        if n % t == 0:
            best = t
    assert best is not None, (n, cap, mult)
    return best


def _dot(a, b):
    return jnp.dot(a, b, preferred_element_type=F32)


def _dot_nt(a, b):
    return lax.dot_general(a, b, (((1,), (1,)), ((), ())), preferred_element_type=F32)


def _dot_tn(a, b):
    return lax.dot_general(a, b, (((0,), (0,)), ((), ())), preferred_element_type=F32)


def _norm_mod(x, g, sc, sh):
    ms = jnp.mean(x * x, axis=-1, keepdims=True)
    return (x * lax.rsqrt(ms + NORM_EPS) * g) * (1.0 + sc) + sh


def _mod_vec(mods_ref, seg, b, row0, tm, n_ctx, n_b):
    cols = slice(seg * D_MODEL, (seg + 1) * D_MODEL)
    if n_ctx % tm == 0:
        r = jnp.where(row0 < n_ctx, n_b, b)
        return mods_ref[pl.ds(r, 1), cols]
    lat = mods_ref[pl.ds(b, 1), cols]
    ctx = mods_ref[pl.ds(n_b, 1), cols]
    rows = row0 + lax.broadcasted_iota(jnp.int32, (tm, 1), 0)
    return jnp.where(rows < n_ctx, ctx, lat)


def _modvec_kernel(c_ref, w_ref, b_ref, o_ref):
    c = c_ref[...]
    s = c * jax.nn.sigmoid(c)
    o_ref[0] = jnp.dot(s, w_ref[0], precision=HIGHEST, preferred_element_type=F32) + b_ref[0]


def _modvecs(cond, w_mod, b_mod):
    depth, d, n = w_mod.shape
    tn = _pick_tile(n, 1536, LANES)
    return pl.pallas_call(
        _modvec_kernel,
        grid=(depth, n // tn),
        in_specs=[pl.BlockSpec((8, d), lambda l, j: (0, 0)),
                  pl.BlockSpec((1, d, tn), lambda l, j: (l, 0, j)),
                  pl.BlockSpec((1, 1, tn), lambda l, j: (l, 0, j))],
        out_specs=pl.BlockSpec((1, 8, tn), lambda l, j: (l, 0, j)),
        out_shape=jax.ShapeDtypeStruct((depth, 8, n), F32),
        compiler_params=_cparams("parallel", "parallel"),
    )(cond, w_mod, b_mod.reshape(depth, 1, n))


def _inproj_kernel(x_ref, mods_ref, g_ref, w_ref, cos_ref, sin_ref, gq_ref, gk_ref,
                   u_ref, qp_ref, kp_ref, vp_ref, qn_ref, kn_ref, vn_ref, gt_ref, *, tm, n_ctx, n_b):
    b = pl.program_id(0)
    row0 = pl.program_id(1) * tm
    sh = _mod_vec(mods_ref, 0, b, row0, tm, n_ctx, n_b)
    sc = _mod_vec(mods_ref, 1, b, row0, tm, n_ctx, n_b)
    h = _norm_mod(x_ref[0], g_ref[...], sc, sh).astype(BF16)
    offs = np.cumsum((0,) + IN_WIDTHS)
    proj = lambda k: _dot(h, w_ref[:, offs[k]:offs[k + 1]])
    u_ref[0] = proj(0)
    for k, o_ref in ((4, qn_ref), (5, kn_ref), (6, vn_ref), (7, gt_ref)):
        o_ref[0] = proj(k).astype(o_ref.dtype)

    cos = cos_ref[...]
    sin = sin_ref[...]
    lane = lax.broadcasted_iota(jnp.int32, (1, LANES), 1)
    first = (lane % (HEAD_DIM // 2)) < (HEAD_DIM // 4)
    low = lane < HEAD_DIM

    def rope(x):
        up = pltpu.roll(x, LANES - HEAD_DIM // 4, 1)
        dn = pltpu.roll(x, HEAD_DIM // 4, 1)
        return x * cos + jnp.where(first, up, dn) * sin

    gq = gq_ref[...] * (HEAD_DIM ** -0.5 * LOG2_E)
    q = proj(1)
    for hd in range(GQA_HEADS):
        x = q[:, hd * LANES:(hd + 1) * LANES]
        ms = jnp.sum(x * x, axis=-1, keepdims=True) * (1.0 / HEAD_DIM)
        qp_ref[0, hd] = rope(x * lax.rsqrt(ms + NORM_EPS) * gq).astype(qp_ref.dtype)
    x = proj(2)
    x2 = x * x
    s_lo = jnp.sum(jnp.where(low, x2, 0.0), axis=-1, keepdims=True)
    s_hi = jnp.sum(jnp.where(low, 0.0, x2), axis=-1, keepdims=True)
    ms = jnp.where(low, s_lo, s_hi) * (1.0 / HEAD_DIM)
    kp_ref[0] = rope(x * lax.rsqrt(ms + NORM_EPS) * gk_ref[...]).astype(kp_ref.dtype)
    v = proj(3)
    vp_ref[0, 0] = jnp.where(low, v, 1.0).astype(vp_ref.dtype)
    vp_ref[0, 1] = jnp.where(low, 1.0, v).astype(vp_ref.dtype)


def _in_proj(xa, mods, g, w_cat, cos_t, sin_t, gq, gk, n_ctx):
    nb, l, d = xa.shape
    tm = _pick_tile(n_ctx, 256, 16)
    n = sum(IN_WIDTHS)
    tile2 = lambda v: jnp.concatenate([v, v]).reshape(1, LANES).astype(F32)
    tok = lambda w: pl.BlockSpec((1, tm, w), lambda b, i: (b, i, 0))
    heads = lambda k: pl.BlockSpec((1, k, tm, LANES), lambda b, i: (b, 0, i, 0))
    tab = pl.BlockSpec((tm, LANES), lambda b, i: (i, 0))
    vec = lambda w: pl.BlockSpec((1, w), lambda b, i: (0, 0))
    act = lambda w, dt=BF16: jax.ShapeDtypeStruct((nb, l, w), dt)
    kern = functools.partial(_inproj_kernel, tm=tm, n_ctx=n_ctx, n_b=nb)
    return pl.pallas_call(
        kern,
        grid=(nb, l // tm),
        in_specs=[tok(d), pl.BlockSpec(mods.shape, lambda b, i: (0, 0)), vec(d),
                  pl.BlockSpec((d, n), lambda b, i: (0, 0)), tab, tab, vec(LANES), vec(LANES)],
        out_specs=[tok(SSM_WIDTH), heads(GQA_HEADS), tok(LANES), heads(GQA_KV_HEADS),
                   tok(NA_WIDTH), tok(NA_WIDTH), tok(NA_WIDTH), tok(N_BRANCH * d)],
        out_shape=[act(SSM_WIDTH, F32), jax.ShapeDtypeStruct((nb, GQA_HEADS, l, LANES), BF16), act(LANES),
                   jax.ShapeDtypeStruct((nb, GQA_KV_HEADS, l, LANES), BF16),
                   act(NA_WIDTH), act(NA_WIDTH), act(NA_WIDTH), act(N_BRANCH * d)],
        compiler_params=_cparams("parallel", "parallel"),
    )(xa, mods, g.reshape(1, d), w_cat, cos_t, sin_t, tile2(gq), tile2(gk))


def _rope_tables(n_ctx, n_lat):
    t = jnp.arange(n_lat)
    pos = jnp.stack([(t // GRID_W).astype(F32), (t % GRID_W).astype(F32)], axis=1)
    n_freq = HEAD_DIM // 4
    inv = ROPE_BASE ** (-jnp.arange(n_freq, dtype=F32) / n_freq)
    lane = np.arange(LANES)
    axis = (lane % HEAD_DIM) // (HEAD_DIM // 2)
    freq = lane % n_freq
    sign = np.where((lane % (HEAD_DIM // 2)) < n_freq, -1.0, 1.0).astype(np.float32)
    ang = pos[:, axis] * inv[freq][None, :]
    cos_t = jnp.concatenate([jnp.ones((n_ctx, LANES), F32), jnp.cos(ang)], axis=0)
    sin_t = jnp.concatenate([jnp.zeros((n_ctx, LANES), F32), jnp.sin(ang) * sign[None, :]], axis=0)
    return cos_t, sin_t


def _gqa_kernel(q_ref, k_ref, v_ref, o_ref, s0_ref, s1_ref, p0_ref, p1_ref, acc_ref, *, tq, tk, n_ctx, n_chunks):
    i = pl.program_id(1)
    per_kv = GQA_HEADS // GQA_KV_HEADS
    half = per_kv * tq
    q = q_ref[0].reshape(GQA_HEADS * tq, LANES)
    s_slots, p_slots = (s0_ref, s1_ref), (p0_ref, p1_ref)

    def scores(rows, n, s_ref):
        s = _dot_nt(k_ref[0, rows, :], q)
        s_ref[0:n, :] = s
        return jnp.max(s, axis=0, keepdims=True)

    def softmax(n, s_ref, p_ref, m_prev, c_max):
        m_new = jnp.maximum(m_prev, c_max)
        p_ref[0:n, :] = jnp.exp2(s_ref[0:n, :] - m_new).astype(BF16)
        return m_new, jnp.exp2(m_prev - m_new)

    def weighted_values(rows, n, p_ref, alpha):
        for j in range(GQA_KV_HEADS):
            cols = slice(j * half, (j + 1) * half)
            upd = _dot_tn(v_ref[0, j, rows, :], p_ref[0:n, cols])
            acc_ref[:, cols] = upd if alpha is None else alpha[:, cols] * acc_ref[:, cols] + upd

    def finalize():
        pieces = []
        for h in range(GQA_HEADS):
            cols = slice(h * tq, (h + 1) * tq)
            j = h // per_kv
            den = acc_ref[(1 - j) * HEAD_DIM:(1 - j) * HEAD_DIM + 1, cols]
            pieces.append(acc_ref[j * HEAD_DIM:(j + 1) * HEAD_DIM, cols] * (1.0 / den))
        o_ref[0] = jnp.concatenate(pieces, axis=0).T.astype(o_ref.dtype)

    m_init = jnp.full((1, GQA_HEADS * tq), NEG_BIG, F32)

    @pl.when(i * tq < n_ctx)
    def _():
        rows = slice(0, n_ctx)
        c_max = scores(rows, n_ctx, s0_ref)
        softmax(n_ctx, s0_ref, p0_ref, m_init, c_max)
        weighted_values(rows, n_ctx, p0_ref, None)
        finalize()

    @pl.when(i * tq >= n_ctx)
    def _():
        chunk = lambda c: slice(c * tk, (c + 1) * tk)
        m, alpha, c_max = m_init, {}, {}
        c_max[0] = scores(chunk(0), tk, s_slots[0])
        for c in range(n_chunks):
            if c + 1 < n_chunks:
                c_max[c + 1] = scores(chunk(c + 1), tk, s_slots[(c + 1) % 2])
            m, alpha[c] = softmax(tk, s_slots[c % 2], p_slots[c % 2], m, c_max[c])
            if c >= 1:
                weighted_values(chunk(c - 1), tk, p_slots[(c - 1) % 2], alpha[c - 1] if c > 1 else None)
        last = n_chunks - 1
        weighted_values(chunk(last), tk, p_slots[last % 2], alpha[last] if last > 0 else None)
        finalize()


def _gqa_attention(qp, kp, vp, n_ctx):
    nb, _, l, _ = qp.shape
    tq = _pick_tile(n_ctx, 256, LANES)
    tk = _pick_tile(l, 768, LANES)
    assert n_ctx <= tk
    kern = functools.partial(_gqa_kernel, tq=tq, tk=tk, n_ctx=n_ctx, n_chunks=l // tk)
    return pl.pallas_call(
        kern,
        grid=(nb, l // tq),
        in_specs=[pl.BlockSpec((1, GQA_HEADS, tq, LANES), lambda b, i: (b, 0, i, 0)),
                  pl.BlockSpec((1, l, LANES), lambda b, i: (b, 0, 0)),
                  pl.BlockSpec((1, GQA_KV_HEADS, l, LANES), lambda b, i: (b, 0, 0, 0))],
        out_specs=pl.BlockSpec((1, tq, GQA_HEADS * HEAD_DIM), lambda b, i: (b, i, 0)),
        out_shape=jax.ShapeDtypeStruct((nb, l, GQA_HEADS * HEAD_DIM), BF16),
        scratch_shapes=[pltpu.VMEM((tk, GQA_HEADS * tq), F32), pltpu.VMEM((tk, GQA_HEADS * tq), F32),
                        pltpu.VMEM((tk, GQA_HEADS * tq), BF16), pltpu.VMEM((tk, GQA_HEADS * tq), BF16),
                        pltpu.VMEM((GQA_KV_WIDTH, GQA_HEADS * tq), F32)],
        compiler_params=_cparams("parallel", "arbitrary"),
    )(qp, kp, vp)


def _na_kernel(var_ref, q_ref, kc_ref, k0_ref, k1_ref, k2_ref, vc_ref, v0_ref, v1_ref, v2_ref, bm_ref, o_ref):
    del var_ref
    i = pl.program_id(1)
    lane = lax.broadcasted_iota(jnp.int32, (1, LANES), 1)

    def attend(pieces):
        def head_scores(h):
            pair, sub = divmod(h, 2)
            cols = slice(pair * LANES, (pair + 1) * LANES)
            own = (lane < HEAD_DIM) if sub == 0 else (lane >= HEAD_DIM)
            qh = (jnp.where(own, q_ref[0, :, cols], 0) * (HEAD_DIM ** -0.5)).astype(BF16)
            scores = []
            for k_ref, _, row0 in pieces:
                s = _dot_nt(k_ref[0, :, cols], qh)
                if row0 is not None:
                    s = s + bm_ref[0, h, row0:row0 + NA_BLOCK, :]
                scores.append(s)
            return scores

        def head_out(h, scores):
            pair, sub = divmod(h, 2)
            cols = slice(pair * LANES, (pair + 1) * LANES)
            m = functools.reduce(jnp.maximum, [jnp.max(s, axis=0, keepdims=True) for s in scores])
            den = 0.0
            acc = 0.0
            for s, (_, v_ref, _) in zip(scores, pieces):
                p = jnp.exp(s - m)
                den = den + jnp.sum(p, axis=0, keepdims=True)
                acc = acc + _dot_tn(v_ref[0, :, cols], p.astype(BF16))
            return acc[sub * HEAD_DIM:(sub + 1) * HEAD_DIM, :] * (1.0 / den)

        outs = []
        scores = head_scores(0)
        for h in range(NA_HEADS):
            nxt = head_scores(h + 1) if h + 1 < NA_HEADS else None
            outs.append(head_out(h, scores))
            scores = nxt
        o_ref[0] = jnp.concatenate(outs, axis=0).T.astype(o_ref.dtype)

    @pl.when(i == 0)
    def _():
        attend([(kc_ref, vc_ref, None)])

    @pl.when(i > 0)
    def _():
        attend([(k0_ref, v0_ref, 0), (k1_ref, v1_ref, NA_BLOCK), (k2_ref, v2_ref, 2 * NA_BLOCK),
                (kc_ref, vc_ref, None)])


def _na_bias_tables(rpb, n_lat):
    rows = n_lat // GRID_W
    n_blk = n_lat // NA_BLOCK
    slab_rows = NA_SLAB_BLOCKS * NA_BLOCK_ROWS
    assert rows >= slab_rows and rows >= NA_WIN_H
    sigs, var_of_blk = [], []
    for ib in range(n_blk):
        r0 = ib * NA_BLOCK_ROWS
        us = NA_BLOCK_ROWS * min(max(ib - 1, 0), n_blk - NA_SLAB_BLOCKS)
        rs = [min(max(r0 + g - NA_WIN_H // 2, 0), rows - NA_WIN_H) for g in range(NA_BLOCK_ROWS)]
        assert us <= min(rs) and max(rs) + NA_WIN_H <= us + slab_rows
        sig = (us - r0, tuple(r - r0 for r in rs))
        if sig not in sigs:
            sigs.append(sig)
        var_of_blk.append(sigs.index(sig))
    i_k = np.arange(slab_rows)[:, None, None, None]
    c_k = np.arange(GRID_W)[None, :, None, None]
    g_q = np.arange(NA_BLOCK_ROWS)[None, None, :, None]
    c_q = np.arange(GRID_W)[None, None, None, :]
    cs = np.clip(c_q - NA_WIN_W // 2, 0, GRID_W - NA_WIN_W)
    col_ok = (c_k >= cs) & (c_k < cs + NA_WIN_W)
    col_off = np.clip(c_k - c_q, -(NA_WIN_W - 1), NA_WIN_W - 1) + (NA_WIN_W - 1)
    n_co = 2 * NA_WIN_W - 1
    onehot = (col_off[0, :, 0, :][None] == np.arange(n_co)[:, None, None]).astype(np.float32)
    toe = jnp.einsum('hrc,ckq->hrkq', rpb.astype(F32), onehot, precision=HIGHEST)
    full = (slab_rows, GRID_W, NA_BLOCK_ROWS, GRID_W)
    tables = []
    for du, drs in sigs:
        drs = np.asarray(drs)[None, None, :, None]
        row_ok = (du + i_k >= drs) & (du + i_k < drs + NA_WIN_H)
        ok = np.broadcast_to(row_ok & col_ok, full).reshape(slab_rows * GRID_W, NA_BLOCK)
        row_off = np.clip(du + i_k[:, 0, :, 0] - g_q[0, 0, :, 0][None, :] + (NA_WIN_H - 1), 0, 2 * NA_WIN_H - 2)
        bias = toe[:, row_off].transpose(0, 1, 3, 2, 4)
        tables.append(jnp.where(ok[None], bias.reshape(rpb.shape[0], slab_rows * GRID_W, NA_BLOCK), NEG_BIG))
    return jnp.stack(tables), jnp.asarray([0] + var_of_blk, jnp.int32)


def _na_attention(qn, kn, vn, bm, var_idx, n_ctx):
    nb, l, w = qn.shape
    assert n_ctx == NA_BLOCK
    n_lat_blk = (l - n_ctx) // NA_BLOCK
    blk = lambda f: pl.BlockSpec((1, NA_BLOCK, w), f)
    slab = lambda d: blk(lambda b, i, var: (b, 1 + jnp.clip(i - 2, 0, n_lat_blk - NA_SLAB_BLOCKS) + d, 0))
    ctx = blk(lambda b, i, var: (b, 0, 0))
    grid_spec = pltpu.PrefetchScalarGridSpec(
        num_scalar_prefetch=1,
        grid=(nb, l // NA_BLOCK),
        in_specs=[blk(lambda b, i, var: (b, i, 0)),
                  ctx, slab(0), slab(1), slab(2),
                  ctx, slab(0), slab(1), slab(2),
                  pl.BlockSpec((1,) + bm.shape[1:], lambda b, i, var: (var[i], 0, 0, 0))],
        out_specs=blk(lambda b, i, var: (b, i, 0)),
    )
    return pl.pallas_call(
        _na_kernel,
        grid_spec=grid_spec,
        out_shape=jax.ShapeDtypeStruct((nb, l, w), BF16),
        compiler_params=_cparams("parallel", "arbitrary"),
    )(var_idx, qn, kn, kn, kn, kn, vn, vn, vn, vn, bm)


def _swap_blocks(tiles, blk):
    n = len(tiles)
    out = [None] * n
    for d in range(n):
        mixed = tiles[d]
        for m in range(1, n):
            mixed = jnp.where(blk == m, tiles[(m + d) % n], mixed)
        if d:
            mixed = pltpu.roll(mixed, d * SSM_GROUP, 1)
        for m in range(n):
            out[m] = mixed if out[m] is None else jnp.where(blk == (m + d) % n, mixed, out[m])
    return out


def _s5_kernel(u_ref, t_ref, w_ref, vf_ref, vb_ref, a_ref, o_ref, ug_ref, xl_ref, xf_ref, xb_ref, y_ref, *, nc, ncc):
    blk = lax.broadcasted_iota(jnp.int32, (1, LANES), 1) // SSM_GROUP
    halves = S5_TILE // LANES
    for half in range(halves):
        rows = [u_ref[0, pl.ds(half * S5_GPT + sl, nc, stride=S5_CHUNK), :] for sl in range(S5_GPT)]
        for g, tile in enumerate(_swap_blocks(rows, blk)):
            ug_ref[g, :, half * LANES:(half + 1) * LANES] = tile.astype(BF16)
    for g in range(S5_GPT):
        xl_ref[g] = _dot(ug_ref[g], w_ref[g])
        y_ref[g] = _dot(ug_ref[g], t_ref[g])

    fwd = lax.broadcasted_iota(jnp.int32, (1, LANES), 1) < SSM_STATE
    re = slice(0, LANES)
    im = slice(LANES, 2 * LANES)
    sub = lax.broadcasted_iota(jnp.int32, (SUBLANES, 1), 0)
    nt, nct = nc // SUBLANES, ncc // SUBLANES

    def body(it, carry):
        tb = jnp.where(it < nct, nct - 1 - it, nt - 1 - (it - nct))
        rows_f = pl.ds(pl.multiple_of(it * SUBLANES, SUBLANES), SUBLANES)
        rows_b = pl.ds(pl.multiple_of(tb * SUBLANES, SUBLANES), SUBLANES)
        new = []
        for g in range(S5_GPT):
            r, m = carry[2 * g], carry[2 * g + 1]
            ar = a_ref[g, 0:1, :]
            ai = a_ref[g, 1:2, :]
            loc_f = xl_ref[g, rows_f, :]
            loc_b = xl_ref[g, rows_b, :]
            ent_f = ent_b = None
            for j in range(SUBLANES):
                jb = SUBLANES - 1 - j
                state = jnp.concatenate([r, m], axis=1)
                ent_f = state if j == 0 else jnp.where(sub == j, state, ent_f)
                ent_b = state if j == 0 else jnp.where(sub == jb, state, ent_b)
                lr = jnp.where(fwd, loc_f[j:j + 1, re], loc_b[jb:jb + 1, re])
                li = jnp.where(fwd, loc_f[j:j + 1, im], loc_b[jb:jb + 1, im])
                r, m = ar * r - ai * m + lr, ar * m + ai * r + li
            xf_ref[g, rows_f, :] = jnp.broadcast_to(ent_f, (SUBLANES, 2 * LANES))
            xb_ref[g, rows_b, :] = jnp.broadcast_to(ent_b, (SUBLANES, 2 * LANES))
            new += [r, m]
        return tuple(new)

    zero = jnp.zeros((1, LANES), F32)
    lax.fori_loop(0, nt, body, (zero,) * (2 * S5_GPT))
    for g in range(S5_GPT):
        y_ref[g] += _dot(xf_ref[g].astype(BF16), vf_ref[g]) + _dot(xb_ref[g].astype(BF16), vb_ref[g])
    for half in range(halves):
        tiles = [y_ref[g, :, half * LANES:(half + 1) * LANES] for g in range(S5_GPT)]
        for tl, tile in enumerate(_swap_blocks(tiles, blk)):
            o_ref[0, pl.ds(half * S5_GPT + tl, nc, stride=S5_CHUNK), :] = tile


def _s5_tables(lam_re, lam_im, log_dt, b_re, b_im, c_re, c_im, d_skip):
    hp = dict(precision=HIGHEST)
    dt = jnp.exp(log_dt.astype(F32))[:, :, None]
    lr, li = lam_re.astype(F32), lam_im.astype(F32)
    k = jnp.arange(S5_CHUNK + 1, dtype=F32)[:, None, None, None]
    mag = jnp.exp(k * (lr * dt)[None])
    pr = mag * jnp.cos(k * (li * dt)[None])
    pi = mag * jnp.sin(k * (li * dt)[None])
    den = lr * lr + li * li
    zr = ((pr[1] - 1.0) * lr + pi[1] * li) / den
    zi = (pi[1] * lr - (pr[1] - 1.0) * li) / den
    br, bi = b_re.astype(F32), b_im.astype(F32)
    bbr = zr[..., None] * br - zi[..., None] * bi
    bbi = zr[..., None] * bi + zi[..., None] * br
    cr, ci = c_re.astype(F32), c_im.astype(F32)
    clr = cr[None] * pr[:, :, :, None, :] - ci[None] * pi[:, :, :, None, :]
    cli = cr[None] * pi[:, :, :, None, :] + ci[None] * pr[:, :, :, None, :]
    kern = (jnp.einsum('kdgop,dgpi->kdgoi', clr, bbr, **hp)
            - jnp.einsum('kdgop,dgpi->kdgoi', cli, bbi, **hp))
    s_i = np.arange(S5_CHUNK)[:, None]
    t_i = np.arange(S5_CHUNK)[None, :]
    lag = t_i - s_i
    kf = jnp.where((lag >= 0)[:, :, None, None, None], kern[np.abs(lag), 0], 0.0)
    kb = jnp.where((lag <= 0)[:, :, None, None, None], kern[np.abs(lag), 1], 0.0)
    eye_h = jnp.eye(SSM_GROUP, dtype=F32)
    dsk = d_skip.astype(F32).reshape(SSM_GROUPS, SSM_GROUP)
    skip = (jnp.eye(S5_CHUNK, dtype=F32)[:, :, None, None, None]
            * (dsk[:, :, None] * eye_h[None])[None, None])
    tmat = (kf + kb + skip).transpose(2, 0, 4, 1, 3).reshape(SSM_GROUPS, S5_TILE, S5_TILE)
    pf_r, pf_i = pr[S5_CHUNK - 1 - np.arange(S5_CHUNK), 0], pi[S5_CHUNK - 1 - np.arange(S5_CHUNK), 0]
    pb_r, pb_i = pr[np.arange(S5_CHUNK), 1], pi[np.arange(S5_CHUNK), 1]

    def state_in(p_r, p_i, d):
        w_r = p_r[:, :, :, None] * bbr[d][None] - p_i[:, :, :, None] * bbi[d][None]
        w_i = p_r[:, :, :, None] * bbi[d][None] + p_i[:, :, :, None] * bbr[d][None]
        to_cols = lambda w: w.transpose(1, 0, 3, 2).reshape(SSM_GROUPS, S5_TILE, SSM_STATE)
        return to_cols(w_r), to_cols(w_i)

    wf_r, wf_i = state_in(pf_r, pf_i, 0)
    wb_r, wb_i = state_in(pb_r, pb_i, 1)
    wmat = jnp.concatenate([wf_r, wb_r, wf_i, wb_i], axis=-1)
    t_f = np.arange(S5_CHUNK) + 1
    t_b = S5_CHUNK - np.arange(S5_CHUNK)
    to_rows = lambda m: m.transpose(1, 3, 0, 2).reshape(SSM_GROUPS, SSM_STATE, S5_TILE)
    zeros = jnp.zeros((SSM_GROUPS, SSM_STATE, S5_TILE), F32)
    vf = jnp.concatenate([to_rows(clr[t_f, 0]), zeros, -to_rows(cli[t_f, 0]), zeros], axis=1)
    vb = jnp.concatenate([zeros, to_rows(clr[t_b, 1]), zeros, -to_rows(cli[t_b, 1])], axis=1)
    a_re = jnp.concatenate([pr[S5_CHUNK, 0], pr[S5_CHUNK, 1]], axis=-1)
    a_im = jnp.concatenate([pi[S5_CHUNK, 0], pi[S5_CHUNK, 1]], axis=-1)
    amat = jnp.concatenate([a_re[:, None], a_im[:, None], jnp.zeros((SSM_GROUPS, 6, LANES), F32)], axis=1)
    return tmat.astype(BF16), wmat.astype(BF16), vf.astype(BF16), vb.astype(BF16), amat


def _s5(u, tables, n_ctx):
    nb, l, w = u.shape
    nc = l // S5_CHUNK
    assert nc % SUBLANES == 0 and (n_ctx // S5_CHUNK) % SUBLANES == 0 and S5_GPT * SSM_GROUP == LANES
    tmat, wmat, vf, vb, amat = tables
    mat = pl.BlockSpec((S5_GPT, S5_TILE, S5_TILE), lambda j, b: (j, 0, 0))
    tok = pl.BlockSpec((1, l, LANES), lambda j, b: (b, 0, j))
    kern = functools.partial(_s5_kernel, nc=nc, ncc=n_ctx // S5_CHUNK)
    return pl.pallas_call(
        kern,
        grid=(w // LANES, nb),
        in_specs=[tok, mat, mat, mat, mat, pl.BlockSpec((S5_GPT, 8, LANES), lambda j, b: (j, 0, 0))],
        out_specs=tok,
        out_shape=jax.ShapeDtypeStruct((nb, l, w), F32),
        scratch_shapes=[pltpu.VMEM((S5_GPT, nc, S5_TILE), BF16)] + [pltpu.VMEM((S5_GPT, nc, S5_TILE), F32)] * 4,
        compiler_params=_cparams("parallel", "parallel"),
    )(u, tmat, wmat, vf, vb, amat)


def _merge_kernel(y_ref, a_ref, n_ref, gt_ref, x_ref, mods_ref, wglu_ref, wbr_ref, wout_ref, o_ref,
                  *, tm, n_ctx, n_b):
    b = pl.program_id(0)
    row0 = pl.program_id(1) * tm
    z = jax.nn.gelu(y_ref[0].astype(F32))
    glu = z * jax.nn.sigmoid(_dot(z.astype(BF16), wglu_ref[...]))
    gate = lambda k: jax.nn.sigmoid(gt_ref[0, :, k * D_MODEL:(k + 1) * D_MODEL].astype(F32))
    merged = gate(0) * _dot(glu.astype(BF16), wbr_ref[0])
    merged = merged + gate(1) * _dot(a_ref[0], wbr_ref[1])
    merged = merged + gate(2) * _dot(n_ref[0], wbr_ref[2])
    y = _dot(merged.astype(BF16), wout_ref[...])
    g1 = _mod_vec(mods_ref, 2, b, row0, tm, n_ctx, n_b)
    o_ref[0] = x_ref[0] + g1 * y


def _merge(y_ssm, gqa, na, gates, xa, mods, w_glu, w_branch, w_out, n_ctx):
    nb, l, d = xa.shape
    tm = _pick_tile(n_ctx, 256, 16)
    tok = lambda w: pl.BlockSpec((1, tm, w), lambda b, i: (b, i, 0))
    full = lambda a: pl.BlockSpec(a.shape, lambda b, i: (0,) * a.ndim)
    kern = functools.partial(_merge_kernel, tm=tm, n_ctx=n_ctx, n_b=nb)
    return pl.pallas_call(
        kern,
        grid=(nb, l // tm),
        in_specs=[tok(SSM_WIDTH), tok(SSM_WIDTH), tok(NA_WIDTH), tok(N_BRANCH * d), tok(d),
                  full(mods), full(w_glu), full(w_branch), full(w_out)],
        out_specs=tok(d),
        out_shape=jax.ShapeDtypeStruct((nb, l, d), F32),
        compiler_params=_cparams("parallel", "parallel"),
    )(y_ssm, gqa, na, gates, xa, mods, w_glu, w_branch, w_out)


def _top2_sum(v):
    hi1, lo1 = jnp.maximum(v[0], v[1]), jnp.minimum(v[0], v[1])
    hi2, lo2 = jnp.maximum(v[2], v[3]), jnp.minimum(v[2], v[3])
    return jnp.maximum(hi1, hi2) + jnp.maximum(jnp.minimum(hi1, hi2), jnp.maximum(lo1, lo2))


def _first_argmax(vals):
    best, idx = vals[0], jnp.zeros(vals[0].shape, jnp.int32)
    for j in range(1, len(vals)):
        upd = vals[j] > best
        best = jnp.where(upd, vals[j], best)
        idx = jnp.where(upd, j, idx)
    return best, idx


def _router_kernel(x_ref, mods_ref, g_ref, rw_ref, rb_ref, o_ref, *, tm, n_ctx, n_b):
    b = pl.program_id(0)
    row0 = pl.program_id(1) * tm
    sh = _mod_vec(mods_ref, 3, b, row0, tm, n_ctx, n_b)
    sc = _mod_vec(mods_ref, 4, b, row0, tm, n_ctx, n_b)
    h = _norm_mod(x_ref[0], g_ref[...], sc, sh)
    logits = lax.dot_general(rw_ref[...], h, (((1,), (1,)), ((), ())), precision=HIGHEST,
                             preferred_element_type=F32)
    rows = [logits[e:e + 1, :] for e in range(N_EXPERTS)]
    mx = functools.reduce(jnp.maximum, rows)
    ex = [jnp.exp(r - mx) for r in rows]
    den = functools.reduce(lambda a, c: a + c, ex)
    probs = [e / den for e in ex]
    sel = [probs[e] + rb_ref[e:e + 1, :] for e in range(N_EXPERTS)]
    grp = [sel[g * EXPERTS_PER_GROUP:(g + 1) * EXPERTS_PER_GROUP] for g in range(N_EXPERT_GROUPS)]
    _, best = _first_argmax([_top2_sum(v) for v in grp])
    pick = lambda table, j: functools.reduce(
        lambda a, g: jnp.where(best == g, table[g * EXPERTS_PER_GROUP + j], a),
        range(1, N_EXPERT_GROUPS), table[j])
    sel_g = [pick(sel, j) for j in range(EXPERTS_PER_GROUP)]
    prob_g = [pick(probs, j) for j in range(EXPERTS_PER_GROUP)]
    _, i1 = _first_argmax(sel_g)
    _, i2 = _first_argmax([jnp.where(i1 == j, -jnp.inf, sel_g[j]) for j in range(EXPERTS_PER_GROUP)])
    at = lambda idx: functools.reduce(lambda a, j: jnp.where(idx == j, prob_g[j], a),
                                      range(1, EXPERTS_PER_GROUP), prob_g[0])
    w1, w2 = at(i1), at(i2)
    tot = w1 + w2
    w1, w2 = w1 / tot, w2 / tot
    out = []
    for e in range(N_EXPERTS):
        g, j = divmod(e, EXPERTS_PER_GROUP)
        c = jnp.where(i1 == j, w1, 0.0) + jnp.where(i2 == j, w2, 0.0)
        out.append(jnp.where(best == g, c, 0.0))
    o_ref[0] = jnp.concatenate(out, axis=0)


def _router(xa, mods, g, router_w, router_b, n_ctx):
    nb, l, d = xa.shape
    tm = _pick_tile(n_ctx, 256, LANES)
    kern = functools.partial(_router_kernel, tm=tm, n_ctx=n_ctx, n_b=nb)
    comb_t = pl.pallas_call(
        kern,
        grid=(nb, l // tm),
        in_specs=[pl.BlockSpec((1, tm, d), lambda b, i: (b, i, 0)),
                  pl.BlockSpec(mods.shape, lambda b, i: (0, 0)),
                  pl.BlockSpec((1, d), lambda b, i: (0, 0)),
                  pl.BlockSpec((N_EXPERTS, d), lambda b, i: (0, 0)),
                  pl.BlockSpec((N_EXPERTS, 1), lambda b, i: (0, 0))],
        out_specs=pl.BlockSpec((1, N_EXPERTS, tm), lambda b, i: (b, 0, i)),
        out_shape=jax.ShapeDtypeStruct((nb, N_EXPERTS, l), F32),
        compiler_params=_cparams("parallel", "parallel"),
    )(xa, mods, g.reshape(1, d), router_w.T.astype(F32), router_b.reshape(N_EXPERTS, 1).astype(F32))
    return comb_t


def _moe_kernel(cnt_ref, x_ref, combt_ref, mods_ref, g_ref, wg_ref, wu_ref, wd_ref, o_ref,
                h_ref, acc_ref, rankt_ref, *, tm, n_ctx, n_b):
    b = pl.program_id(0)
    i = pl.program_id(1)
    ep = pl.program_id(2)
    row0 = i * tm

    @pl.when(ep == 0)
    def _():
        sh = _mod_vec(mods_ref, 3, b, row0, tm, n_ctx, n_b)
        sc = _mod_vec(mods_ref, 4, b, row0, tm, n_ctx, n_b)
        h_ref[...] = _norm_mod(x_ref[0], g_ref[...], sc, sh).astype(BF16)
        acc_ref[...] = jnp.zeros(acc_ref.shape, F32)
        routed = jnp.where(combt_ref[0] != 0.0, 1.0, 0.0)
        pos = lax.broadcasted_iota(jnp.int32, (1, tm), 1)
        count, shift = routed, 1
        while shift < tm:
            count = count + jnp.where(pos >= shift, pltpu.roll(count, shift, 1), 0.0)
            shift *= 2
        rankt_ref[...] = count - routed

    base = ((b * pl.num_programs(1) + i) * MOE_PAIRS + ep) * MOE_PAIR
    n_slots = functools.reduce(jnp.maximum, [cnt_ref[base + k] for k in range(MOE_PAIR)])
    slot_r = lax.broadcasted_iota(jnp.int32, (MOE_CAP, 1), 0).astype(F32)
    rows = [(rankt_ref[pl.ds(ep * MOE_PAIR + k, 1), :], combt_ref[0, pl.ds(ep * MOE_PAIR + k, 1), :])
            for k in range(MOE_PAIR)]

    def body(j, carry):
        first = (j * MOE_CAP).astype(F32)
        sel = [(rank == first + slot_r) & (w != 0.0) for rank, w in rows]
        one_hot = jnp.concatenate([jnp.where(m, 1.0, 0.0).astype(BF16) for m in sel], axis=0)
        xc = _dot(one_hot, h_ref[...]).astype(BF16)
        ys = []
        for k in range(MOE_PAIR):
            xk = xc[k * MOE_CAP:(k + 1) * MOE_CAP]
            gate = _dot(xk, wg_ref[k])
            act = (gate * jax.nn.sigmoid(gate)) * _dot(xk, wu_ref[k])
            y = _dot(act.astype(BF16), wd_ref[k])
            weight = jnp.sum(jnp.where(sel[k], rows[k][1], 0.0), axis=1, keepdims=True)
            ys.append((y * weight).astype(BF16))
        acc_ref[...] += _dot_tn(one_hot, jnp.concatenate(ys, axis=0))
        return carry

    lax.fori_loop(0, (n_slots + MOE_CAP - 1) // MOE_CAP, body, 0)

    @pl.when(ep == MOE_PAIRS - 1)
    def _():
        g2 = _mod_vec(mods_ref, 5, b, row0, tm, n_ctx, n_b)
        o_ref[0] = x_ref[0] + g2 * acc_ref[...]


def _moe(xa, comb_t, mods, g, w_gate, w_up, w_down, n_ctx):
    nb, l, d = xa.shape
    tm = _pick_tile(l, 768, LANES)
    nt = l // tm
    counts = (comb_t != 0.0).reshape(nb, N_EXPERTS, nt, tm).sum(axis=-1).astype(jnp.int32)
    counts = counts.transpose(0, 2, 1).reshape(-1)
    kern = functools.partial(_moe_kernel, tm=tm, n_ctx=n_ctx, n_b=nb)
    grid_spec = pltpu.PrefetchScalarGridSpec(
        num_scalar_prefetch=1,
        grid=(nb, nt, MOE_PAIRS),
        in_specs=[pl.BlockSpec((1, tm, d), lambda b, i, e, cnt: (b, i, 0)),
                  pl.BlockSpec((1, N_EXPERTS, tm), lambda b, i, e, cnt: (b, 0, i)),
                  pl.BlockSpec(mods.shape, lambda b, i, e, cnt: (0, 0)),
                  pl.BlockSpec((1, d), lambda b, i, e, cnt: (0, 0)),
                  pl.BlockSpec((MOE_PAIR, d, EXPERT_FF), lambda b, i, e, cnt: (e, 0, 0)),
                  pl.BlockSpec((MOE_PAIR, d, EXPERT_FF), lambda b, i, e, cnt: (e, 0, 0)),
                  pl.BlockSpec((MOE_PAIR, EXPERT_FF, d), lambda b, i, e, cnt: (e, 0, 0))],
        out_specs=pl.BlockSpec((1, tm, d), lambda b, i, e, cnt: (b, i, 0)),
        scratch_shapes=[pltpu.VMEM((tm, d), BF16), pltpu.VMEM((tm, d), F32), pltpu.VMEM((N_EXPERTS, tm), F32)],
    )
    return pl.pallas_call(
        kern,
        grid_spec=grid_spec,
        out_shape=jax.ShapeDtypeStruct((nb, l, d), F32),
        compiler_params=_cparams("parallel", "parallel", "arbitrary"),
    )(counts, xa, comb_t, mods, g.reshape(1, d), w_gate, w_up, w_down)


def _final_norm_kernel(x_ref, g_ref, o_ref):
    x = x_ref[0]
    ms = jnp.mean(x * x, axis=-1, keepdims=True)
    o_ref[0] = x * lax.rsqrt(ms + NORM_EPS) * g_ref[...]


def _final_norm(xa, g, n_ctx):
    nb, l, d = xa.shape
    tm = _pick_tile(n_ctx, 256, 8)
    off = n_ctx // tm
    return pl.pallas_call(
        _final_norm_kernel,
        grid=(nb, (l - n_ctx) // tm),
        in_specs=[pl.BlockSpec((1, tm, d), lambda b, i: (b, i + off, 0)),
                  pl.BlockSpec((1, d), lambda b, i: (0, 0))],
        out_specs=pl.BlockSpec((1, tm, d), lambda b, i: (b, i, 0)),
        out_shape=jax.ShapeDtypeStruct((nb, l - n_ctx, d), F32),
        compiler_params=_cparams("parallel", "parallel"),
    )(xa, g.reshape(1, d).astype(F32))


def _in_proj_weights(w_in):
    w_u, w_qa, w_rest = jnp.split(w_in, [SSM_WIDTH, SSM_WIDTH + GQA_HEADS * HEAD_DIM], axis=-1)
    per_kv = GQA_HEADS // GQA_KV_HEADS
    zero = jnp.zeros(w_in.shape[:-1] + (HEAD_DIM,), w_in.dtype)
    tiles = []
    for h in range(GQA_HEADS):
        w_h = w_qa[..., h * HEAD_DIM:(h + 1) * HEAD_DIM]
        tiles += [w_h, zero] if h // per_kv == 0 else [zero, w_h]
    return jnp.concatenate([w_u] + tiles + [w_rest], axis=-1).astype(BF16)


def kernel(x, c, ctx, c_ctx, w_mod, b_mod, norm1_g, norm2_g, w_in, ssm_lam_re, ssm_lam_im, ssm_log_dt, ssm_b_re, ssm_b_im, ssm_c_re, ssm_c_im, ssm_d, ssm_w_glu, gqa_q_norm_g, gqa_k_norm_g, na_rpb, w_branch, w_out, router_w, router_b, moe_w_gate, moe_w_up, moe_w_down, final_norm_g):
    nb, n_lat, d = x.shape
    n_ctx = ctx.shape[1]
    depth = w_mod.shape[0]
    assert d == D_MODEL and nb < 8 and GQA_KV_HEADS == 2 and sum(IN_WIDTHS) % LANES == 0
    assert n_ctx % S5_CHUNK == 0 and n_lat % NA_BLOCK == 0

    xa = jnp.concatenate([ctx, x], axis=1).astype(F32)
    cond = jnp.zeros((8, d), F32).at[:nb].set(c.astype(F32)).at[nb].set(c_ctx.astype(F32))
    mods = _modvecs(cond, w_mod.astype(F32), b_mod.astype(F32))
    cos_t, sin_t = _rope_tables(n_ctx, n_lat)
    w_cat = _in_proj_weights(w_in)
    s5_tables = jax.vmap(_s5_tables)(ssm_lam_re, ssm_lam_im, ssm_log_dt, ssm_b_re, ssm_b_im, ssm_c_re, ssm_c_im,
                                     ssm_d)
    bm, var_idx = _na_bias_tables(na_rpb.reshape((depth * NA_HEADS,) + na_rpb.shape[2:]), n_lat)
    bm = bm.reshape((bm.shape[0], depth, NA_HEADS) + bm.shape[2:])
    w_glu, w_br, w_o = ssm_w_glu.astype(BF16), w_branch.astype(BF16), w_out.astype(BF16)
    w_g, w_u, w_d = moe_w_gate.astype(BF16), moe_w_up.astype(BF16), moe_w_down.astype(BF16)
    n1, n2 = norm1_g.astype(F32), norm2_g.astype(F32)

    for layer in range(depth):
        u, qp, kp, vp, qn, kn, vn, gates = _in_proj(xa, mods[layer], n1[layer], w_cat[layer], cos_t, sin_t,
                                                    gqa_q_norm_g[layer], gqa_k_norm_g[layer], n_ctx)
        y_ssm = _s5(u, [t[layer] for t in s5_tables], n_ctx)
        gqa = _gqa_attention(qp, kp, vp, n_ctx)
        na = _na_attention(qn, kn, vn, bm[:, layer], var_idx, n_ctx)
        xa = _merge(y_ssm, gqa, na, gates, xa, mods[layer], w_glu[layer], w_br[layer], w_o[layer], n_ctx)
        comb_t = _router(xa, mods[layer], n2[layer], router_w, router_b, n_ctx)
        xa = _moe(xa, comb_t, mods[layer], n2[layer], w_g[layer], w_u[layer], w_d[layer], n_ctx)
    return _final_norm(xa, final_norm_g, n_ctx)
```

```python
import functools
import math

import numpy as np
import jax
import jax.numpy as jnp
from jax import lax
from jax.experimental import pallas as pl
from jax.experimental.pallas import tpu as pltpu

D_MODEL = 1024
GRID_W = 64
HEAD_DIM = 64
SSM_WIDTH = 512
SSM_GROUP = 16
SSM_GROUPS = SSM_WIDTH // SSM_GROUP
SSM_STATE = 64
GQA_HEADS = 8
GQA_KV_HEADS = 2
GQA_KV_WIDTH = GQA_KV_HEADS * HEAD_DIM
NA_HEADS = 8
NA_WIDTH = NA_HEADS * HEAD_DIM
NA_WIN_H = 8
NA_WIN_W = 16
ROPE_BASE = 10000.0
N_BRANCH = 3
N_EXPERTS = 16
N_EXPERT_GROUPS = 4
EXPERTS_PER_GROUP = N_EXPERTS // N_EXPERT_GROUPS
EXPERT_FF = 512
NORM_EPS = 1e-6

LANES = 128
SUBLANES = 8
S5_CHUNK = 16
S5_TILE = S5_CHUNK * SSM_GROUP
S5_GPT = 8
NA_BLOCK_ROWS = 4
NA_BLOCK = NA_BLOCK_ROWS * GRID_W
MOE_PAIR = 2
MOE_PAIRS = N_EXPERTS // MOE_PAIR
MOE_CAP = 256
NA_SLAB_BLOCKS = 3
NEG_BIG = -1e30
LOG2_E = math.log2(math.e)
VMEM_LIMIT = 56 * 1024 * 1024

F32 = jnp.float32
BF16 = jnp.bfloat16
HIGHEST = lax.Precision.HIGHEST

IN_WIDTHS = (SSM_WIDTH, GQA_HEADS * LANES, GQA_KV_WIDTH, GQA_KV_WIDTH, NA_WIDTH, NA_WIDTH, NA_WIDTH,
             N_BRANCH * D_MODEL)


def _cparams(*sem):
    return pltpu.CompilerParams(dimension_semantics=sem, vmem_limit_bytes=VMEM_LIMIT)


def _pick_tile(n, cap, mult):
    best = None
    for t in range(mult, min(n, cap) + 1, mult):
        if n % t == 0:
            best = t
    assert best is not None, (n, cap, mult)
    return best


def _dot(a, b):
    return jnp.dot(a, b, preferred_element_type=F32)


def _dot_nt(a, b):
    return lax.dot_general(a, b, (((1,), (1,)), ((), ())), preferred_element_type=F32)


def _dot_tn(a, b):
    return lax.dot_general(a, b, (((0,), (0,)), ((), ())), preferred_element_type=F32)


def _norm_mod(x, g, sc, sh):
    ms = jnp.mean(x * x, axis=-1, keepdims=True)
    return (x * lax.rsqrt(ms + NORM_EPS) * g) * (1.0 + sc) + sh


def _mod_vec(mods_ref, seg, b, row0, tm, n_ctx, n_b):
    cols = slice(seg * D_MODEL, (seg + 1) * D_MODEL)
    if n_ctx % tm == 0:
        r = jnp.where(row0 < n_ctx, n_b, b)
        return mods_ref[pl.ds(r, 1), cols]
    lat = mods_ref[pl.ds(b, 1), cols]
    ctx = mods_ref[pl.ds(n_b, 1), cols]
    rows = row0 + lax.broadcasted_iota(jnp.int32, (tm, 1), 0)
    return jnp.where(rows < n_ctx, ctx, lat)


def _modvec_kernel(c_ref, w_ref, b_ref, o_ref):
    c = c_ref[...]
    s = c * jax.nn.sigmoid(c)
    o_ref[0] = jnp.dot(s, w_ref[0], precision=HIGHEST, preferred_element_type=F32) + b_ref[0]


def _modvecs(cond, w_mod, b_mod):
    depth, d, n = w_mod.shape
    tn = _pick_tile(n, 1536, LANES)
    return pl.pallas_call(
        _modvec_kernel,
        grid=(depth, n // tn),
        in_specs=[pl.BlockSpec((8, d), lambda l, j: (0, 0)),
                  pl.BlockSpec((1, d, tn), lambda l, j: (l, 0, j)),
                  pl.BlockSpec((1, 1, tn), lambda l, j: (l, 0, j))],
        out_specs=pl.BlockSpec((1, 8, tn), lambda l, j: (l, 0, j)),
        out_shape=jax.ShapeDtypeStruct((depth, 8, n), F32),
        compiler_params=_cparams("parallel", "parallel"),
    )(cond, w_mod, b_mod.reshape(depth, 1, n))


def _inproj_kernel(x_ref, mods_ref, g_ref, w_ref, cos_ref, sin_ref, gq_ref, gk_ref,
                   u_ref, qp_ref, kp_ref, vp_ref, qn_ref, kn_ref, vn_ref, gt_ref, *, tm, n_ctx, n_b):
    b = pl.program_id(0)
    row0 = pl.program_id(1) * tm
    sh = _mod_vec(mods_ref, 0, b, row0, tm, n_ctx, n_b)
    sc = _mod_vec(mods_ref, 1, b, row0, tm, n_ctx, n_b)
    h = _norm_mod(x_ref[0], g_ref[...], sc, sh).astype(BF16)
    offs = np.cumsum((0,) + IN_WIDTHS)
    proj = lambda k: _dot(h, w_ref[:, offs[k]:offs[k + 1]])
    u_ref[0] = proj(0)
    for k, o_ref in ((4, qn_ref), (5, kn_ref), (6, vn_ref), (7, gt_ref)):
        o_ref[0] = proj(k).astype(o_ref.dtype)

    cos = cos_ref[...]
    sin = sin_ref[...]
    lane = lax.broadcasted_iota(jnp.int32, (1, LANES), 1)
    first = (lane % (HEAD_DIM // 2)) < (HEAD_DIM // 4)
    low = lane < HEAD_DIM

    def rope(x):
        up = pltpu.roll(x, LANES - HEAD_DIM // 4, 1)
        dn = pltpu.roll(x, HEAD_DIM // 4, 1)
        return x * cos + jnp.where(first, up, dn) * sin

    gq = gq_ref[...] * (HEAD_DIM ** -0.5 * LOG2_E)
    q = proj(1)
    for hd in range(GQA_HEADS):
        x = q[:, hd * LANES:(hd + 1) * LANES]
        ms = jnp.sum(x * x, axis=-1, keepdims=True) * (1.0 / HEAD_DIM)
        qp_ref[0, hd] = rope(x * lax.rsqrt(ms + NORM_EPS) * gq).astype(qp_ref.dtype)
    x = proj(2)
    x2 = x * x
    s_lo = jnp.sum(jnp.where(low, x2, 0.0), axis=-1, keepdims=True)
    s_hi = jnp.sum(jnp.where(low, 0.0, x2), axis=-1, keepdims=True)
    ms = jnp.where(low, s_lo, s_hi) * (1.0 / HEAD_DIM)
    kp_ref[0] = rope(x * lax.rsqrt(ms + NORM_EPS) * gk_ref[...]).astype(kp_ref.dtype)
    v = proj(3)
    vp_ref[0, 0] = jnp.where(low, v, 1.0).astype(vp_ref.dtype)
    vp_ref[0, 1] = jnp.where(low, 1.0, v).astype(vp_ref.dtype)


def _in_proj(xa, mods, g, w_cat, cos_t, sin_t, gq, gk, n_ctx):
    nb, l, d = xa.shape
    tm = _pick_tile(n_ctx, 256, 16)
    n = sum(IN_WIDTHS)
    tile2 = lambda v: jnp.concatenate([v, v]).reshape(1, LANES).astype(F32)
    tok = lambda w: pl.BlockSpec((1, tm, w), lambda b, i: (b, i, 0))
    heads = lambda k: pl.BlockSpec((1, k, tm, LANES), lambda b, i: (b, 0, i, 0))
    tab = pl.BlockSpec((tm, LANES), lambda b, i: (i, 0))
    vec = lambda w: pl.BlockSpec((1, w), lambda b, i: (0, 0))
    act = lambda w, dt=BF16: jax.ShapeDtypeStruct((nb, l, w), dt)
    kern = functools.partial(_inproj_kernel, tm=tm, n_ctx=n_ctx, n_b=nb)
    return pl.pallas_call(
        kern,
        grid=(nb, l // tm),
        in_specs=[tok(d), pl.BlockSpec(mods.shape, lambda b, i: (0, 0)), vec(d),
                  pl.BlockSpec((d, n), lambda b, i: (0, 0)), tab, tab, vec(LANES), vec(LANES)],
        out_specs=[tok(SSM_WIDTH), heads(GQA_HEADS), tok(LANES), heads(GQA_KV_HEADS),
                   tok(NA_WIDTH), tok(NA_WIDTH), tok(NA_WIDTH), tok(N_BRANCH * d)],
        out_shape=[act(SSM_WIDTH, F32), jax.ShapeDtypeStruct((nb, GQA_HEADS, l, LANES), BF16), act(LANES),
                   jax.ShapeDtypeStruct((nb, GQA_KV_HEADS, l, LANES), BF16),
                   act(NA_WIDTH), act(NA_WIDTH), act(NA_WIDTH), act(N_BRANCH * d)],
        compiler_params=_cparams("parallel", "parallel"),
    )(xa, mods, g.reshape(1, d), w_cat, cos_t, sin_t, tile2(gq), tile2(gk))


def _rope_tables(n_ctx, n_lat):
    t = jnp.arange(n_lat)
    pos = jnp.stack([(t // GRID_W).astype(F32), (t % GRID_W).astype(F32)], axis=1)
    n_freq = HEAD_DIM // 4
    inv = ROPE_BASE ** (-jnp.arange(n_freq, dtype=F32) / n_freq)
    lane = np.arange(LANES)
    axis = (lane % HEAD_DIM) // (HEAD_DIM // 2)
    freq = lane % n_freq
    sign = np.where((lane % (HEAD_DIM // 2)) < n_freq, -1.0, 1.0).astype(np.float32)
    ang = pos[:, axis] * inv[freq][None, :]
    cos_t = jnp.concatenate([jnp.ones((n_ctx, LANES), F32), jnp.cos(ang)], axis=0)
    sin_t = jnp.concatenate([jnp.zeros((n_ctx, LANES), F32), jnp.sin(ang) * sign[None, :]], axis=0)
    return cos_t, sin_t


def _gqa_kernel(q_ref, k_ref, v_ref, o_ref, s0_ref, s1_ref, p0_ref, p1_ref, acc_ref, *, tq, tk, n_ctx, n_chunks):
    i = pl.program_id(1)
    per_kv = GQA_HEADS // GQA_KV_HEADS
    half = per_kv * tq
    q = q_ref[0].reshape(GQA_HEADS * tq, LANES)
    s_slots, p_slots = (s0_ref, s1_ref), (p0_ref, p1_ref)

    def scores(rows, n, s_ref):
        s = _dot_nt(k_ref[0, rows, :], q)
        s_ref[0:n, :] = s
        return jnp.max(s, axis=0, keepdims=True)

    def softmax(n, s_ref, p_ref, m_prev, c_max):
        m_new = jnp.maximum(m_prev, c_max)
        p_ref[0:n, :] = jnp.exp2(s_ref[0:n, :] - m_new).astype(BF16)
        return m_new, jnp.exp2(m_prev - m_new)

    def weighted_values(rows, n, p_ref, alpha):
        for j in range(GQA_KV_HEADS):
            cols = slice(j * half, (j + 1) * half)
            upd = _dot_tn(v_ref[0, j, rows, :], p_ref[0:n, cols])
            acc_ref[:, cols] = upd if alpha is None else alpha[:, cols] * acc_ref[:, cols] + upd

    def finalize():
        pieces = []
        for h in range(GQA_HEADS):
            cols = slice(h * tq, (h + 1) * tq)
            j = h // per_kv
            den = acc_ref[(1 - j) * HEAD_DIM:(1 - j) * HEAD_DIM + 1, cols]
            pieces.append(acc_ref[j * HEAD_DIM:(j + 1) * HEAD_DIM, cols] * (1.0 / den))
        o_ref[0] = jnp.concatenate(pieces, axis=0).T.astype(o_ref.dtype)

    m_init = jnp.full((1, GQA_HEADS * tq), NEG_BIG, F32)

    @pl.when(i * tq < n_ctx)
    def _():
        rows = slice(0, n_ctx)
        c_max = scores(rows, n_ctx, s0_ref)
        softmax(n_ctx, s0_ref, p0_ref, m_init, c_max)
        weighted_values(rows, n_ctx, p0_ref, None)
        finalize()

    @pl.when(i * tq >= n_ctx)
    def _():
        chunk = lambda c: slice(c * tk, (c + 1) * tk)
        m, alpha, c_max = m_init, {}, {}
        c_max[0] = scores(chunk(0), tk, s_slots[0])
        for c in range(n_chunks):
            if c + 1 < n_chunks:
                c_max[c + 1] = scores(chunk(c + 1), tk, s_slots[(c + 1) % 2])
            m, alpha[c] = softmax(tk, s_slots[c % 2], p_slots[c % 2], m, c_max[c])
            if c >= 1:
                weighted_values(chunk(c - 1), tk, p_slots[(c - 1) % 2], alpha[c - 1] if c > 1 else None)
        last = n_chunks - 1
        weighted_values(chunk(last), tk, p_slots[last % 2], alpha[last] if last > 0 else None)
        finalize()


def _gqa_attention(qp, kp, vp, n_ctx):
    nb, _, l, _ = qp.shape
    tq = _pick_tile(n_ctx, 256, LANES)
    tk = _pick_tile(l, 768, LANES)
    assert n_ctx <= tk
    kern = functools.partial(_gqa_kernel, tq=tq, tk=tk, n_ctx=n_ctx, n_chunks=l // tk)
    return pl.pallas_call(
        kern,
        grid=(nb, l // tq),
        in_specs=[pl.BlockSpec((1, GQA_HEADS, tq, LANES), lambda b, i: (b, 0, i, 0)),
                  pl.BlockSpec((1, l, LANES), lambda b, i: (b, 0, 0)),
                  pl.BlockSpec((1, GQA_KV_HEADS, l, LANES), lambda b, i: (b, 0, 0, 0))],
        out_specs=pl.BlockSpec((1, tq, GQA_HEADS * HEAD_DIM), lambda b, i: (b, i, 0)),
        out_shape=jax.ShapeDtypeStruct((nb, l, GQA_HEADS * HEAD_DIM), BF16),
        scratch_shapes=[pltpu.VMEM((tk, GQA_HEADS * tq), F32), pltpu.VMEM((tk, GQA_HEADS * tq), F32),
                        pltpu.VMEM((tk, GQA_HEADS * tq), BF16), pltpu.VMEM((tk, GQA_HEADS * tq), BF16),
                        pltpu.VMEM((GQA_KV_WIDTH, GQA_HEADS * tq), F32)],
        compiler_params=_cparams("parallel", "arbitrary"),
    )(qp, kp, vp)


def _na_kernel(var_ref, q_ref, kc_ref, k0_ref, k1_ref, k2_ref, vc_ref, v0_ref, v1_ref, v2_ref, bm_ref, o_ref):
    del var_ref
    i = pl.program_id(1)
    lane = lax.broadcasted_iota(jnp.int32, (1, LANES), 1)

    def attend(pieces):
        def head_scores(h):
            pair, sub = divmod(h, 2)
            cols = slice(pair * LANES, (pair + 1) * LANES)
            own = (lane < HEAD_DIM) if sub == 0 else (lane >= HEAD_DIM)
            qh = (jnp.where(own, q_ref[0, :, cols], 0) * (HEAD_DIM ** -0.5)).astype(BF16)
            scores = []
            for k_ref, _, row0 in pieces:
                s = _dot_nt(k_ref[0, :, cols], qh)
                if row0 is not None:
                    s = s + bm_ref[0, h, row0:row0 + NA_BLOCK, :]
                scores.append(s)
            return scores

        def head_out(h, scores):
            pair, sub = divmod(h, 2)
            cols = slice(pair * LANES, (pair + 1) * LANES)
            m = functools.reduce(jnp.maximum, [jnp.max(s, axis=0, keepdims=True) for s in scores])
            den = 0.0
            acc = 0.0
            for s, (_, v_ref, _) in zip(scores, pieces):
                p = jnp.exp(s - m)
                den = den + jnp.sum(p, axis=0, keepdims=True)
                acc = acc + _dot_tn(v_ref[0, :, cols], p.astype(BF16))
            return acc[sub * HEAD_DIM:(sub + 1) * HEAD_DIM, :] * (1.0 / den)

        outs = []
        scores = head_scores(0)
        for h in range(NA_HEADS):
            nxt = head_scores(h + 1) if h + 1 < NA_HEADS else None
            outs.append(head_out(h, scores))
            scores = nxt
        o_ref[0] = jnp.concatenate(outs, axis=0).T.astype(o_ref.dtype)

    @pl.when(i == 0)
    def _():
        attend([(kc_ref, vc_ref, None)])

    @pl.when(i > 0)
    def _():
        attend([(k0_ref, v0_ref, 0), (k1_ref, v1_ref, NA_BLOCK), (k2_ref, v2_ref, 2 * NA_BLOCK),
                (kc_ref, vc_ref, None)])


def _na_bias_tables(rpb, n_lat):
    rows = n_lat // GRID_W
    n_blk = n_lat // NA_BLOCK
    slab_rows = NA_SLAB_BLOCKS * NA_BLOCK_ROWS
    assert rows >= slab_rows and rows >= NA_WIN_H
    sigs, var_of_blk = [], []
    for ib in range(n_blk):
        r0 = ib * NA_BLOCK_ROWS
        us = NA_BLOCK_ROWS * min(max(ib - 1, 0), n_blk - NA_SLAB_BLOCKS)
        rs = [min(max(r0 + g - NA_WIN_H // 2, 0), rows - NA_WIN_H) for g in range(NA_BLOCK_ROWS)]
        assert us <= min(rs) and max(rs) + NA_WIN_H <= us + slab_rows
        sig = (us - r0, tuple(r - r0 for r in rs))
        if sig not in sigs:
            sigs.append(sig)
        var_of_blk.append(sigs.index(sig))
    i_k = np.arange(slab_rows)[:, None, None, None]
    c_k = np.arange(GRID_W)[None, :, None, None]
    g_q = np.arange(NA_BLOCK_ROWS)[None, None, :, None]
    c_q = np.arange(GRID_W)[None, None, None, :]
    cs = np.clip(c_q - NA_WIN_W // 2, 0, GRID_W - NA_WIN_W)
    col_ok = (c_k >= cs) & (c_k < cs + NA_WIN_W)
    col_off = np.clip(c_k - c_q, -(NA_WIN_W - 1), NA_WIN_W - 1) + (NA_WIN_W - 1)
    n_co = 2 * NA_WIN_W - 1
    onehot = (col_off[0, :, 0, :][None] == np.arange(n_co)[:, None, None]).astype(np.float32)
    toe = jnp.einsum('hrc,ckq->hrkq', rpb.astype(F32), onehot, precision=HIGHEST)
    full = (slab_rows, GRID_W, NA_BLOCK_ROWS, GRID_W)
    tables = []
    for du, drs in sigs:
        drs = np.asarray(drs)[None, None, :, None]
        row_ok = (du + i_k >= drs) & (du + i_k < drs + NA_WIN_H)
        ok = np.broadcast_to(row_ok & col_ok, full).reshape(slab_rows * GRID_W, NA_BLOCK)
        row_off = np.clip(du + i_k[:, 0, :, 0] - g_q[0, 0, :, 0][None, :] + (NA_WIN_H - 1), 0, 2 * NA_WIN_H - 2)
        bias = toe[:, row_off].transpose(0, 1, 3, 2, 4)
        tables.append(jnp.where(ok[None], bias.reshape(rpb.shape[0], slab_rows * GRID_W, NA_BLOCK), NEG_BIG))
    return jnp.stack(tables), jnp.asarray([0] + var_of_blk, jnp.int32)


def _na_attention(qn, kn, vn, bm, var_idx, n_ctx):
    nb, l, w = qn.shape
    assert n_ctx == NA_BLOCK
    n_lat_blk = (l - n_ctx) // NA_BLOCK
    blk = lambda f: pl.BlockSpec((1, NA_BLOCK, w), f)
    slab = lambda d: blk(lambda b, i, var: (b, 1 + jnp.clip(i - 2, 0, n_lat_blk - NA_SLAB_BLOCKS) + d, 0))
    ctx = blk(lambda b, i, var: (b, 0, 0))
    grid_spec = pltpu.PrefetchScalarGridSpec(
        num_scalar_prefetch=1,
        grid=(nb, l // NA_BLOCK),
        in_specs=[blk(lambda b, i, var: (b, i, 0)),
                  ctx, slab(0), slab(1), slab(2),
                  ctx, slab(0), slab(1), slab(2),
                  pl.BlockSpec((1,) + bm.shape[1:], lambda b, i, var: (var[i], 0, 0, 0))],
        out_specs=blk(lambda b, i, var: (b, i, 0)),
    )
    return pl.pallas_call(
        _na_kernel,
        grid_spec=grid_spec,
        out_shape=jax.ShapeDtypeStruct((nb, l, w), BF16),
        compiler_params=_cparams("parallel", "arbitrary"),
    )(var_idx, qn, kn, kn, kn, kn, vn, vn, vn, vn, bm)


def _swap_blocks(tiles, blk):
    n = len(tiles)
    out = [None] * n
    for d in range(n):
        mixed = tiles[d]
        for m in range(1, n):
            mixed = jnp.where(blk == m, tiles[(m + d) % n], mixed)
        if d:
            mixed = pltpu.roll(mixed, d * SSM_GROUP, 1)
        for m in range(n):
            out[m] = mixed if out[m] is None else jnp.where(blk == (m + d) % n, mixed, out[m])
    return out


def _s5_kernel(u_ref, t_ref, w_ref, vf_ref, vb_ref, a_ref, o_ref, ug_ref, xl_ref, xf_ref, xb_ref, y_ref, *, nc, ncc):
    blk = lax.broadcasted_iota(jnp.int32, (1, LANES), 1) // SSM_GROUP
    halves = S5_TILE // LANES
    for half in range(halves):
        rows = [u_ref[0, pl.ds(half * S5_GPT + sl, nc, stride=S5_CHUNK), :] for sl in range(S5_GPT)]
        for g, tile in enumerate(_swap_blocks(rows, blk)):
            ug_ref[g, :, half * LANES:(half + 1) * LANES] = tile.astype(BF16)
    for g in range(S5_GPT):
        xl_ref[g] = _dot(ug_ref[g], w_ref[g])
        y_ref[g] = _dot(ug_ref[g], t_ref[g])

    fwd = lax.broadcasted_iota(jnp.int32, (1, LANES), 1) < SSM_STATE
    re = slice(0, LANES)
    im = slice(LANES, 2 * LANES)
    sub = lax.broadcasted_iota(jnp.int32, (SUBLANES, 1), 0)
    nt, nct = nc // SUBLANES, ncc // SUBLANES

    def body(it, carry):
        tb = jnp.where(it < nct, nct - 1 - it, nt - 1 - (it - nct))
        rows_f = pl.ds(pl.multiple_of(it * SUBLANES, SUBLANES), SUBLANES)
        rows_b = pl.ds(pl.multiple_of(tb * SUBLANES, SUBLANES), SUBLANES)
        new = []
        for g in range(S5_GPT):
            r, m = carry[2 * g], carry[2 * g + 1]
            ar = a_ref[g, 0:1, :]
            ai = a_ref[g, 1:2, :]
            loc_f = xl_ref[g, rows_f, :]
            loc_b = xl_ref[g, rows_b, :]
            ent_f = ent_b = None
            for j in range(SUBLANES):
                jb = SUBLANES - 1 - j
                state = jnp.concatenate([r, m], axis=1)
                ent_f = state if j == 0 else jnp.where(sub == j, state, ent_f)
                ent_b = state if j == 0 else jnp.where(sub == jb, state, ent_b)
                lr = jnp.where(fwd, loc_f[j:j + 1, re], loc_b[jb:jb + 1, re])
                li = jnp.where(fwd, loc_f[j:j + 1, im], loc_b[jb:jb + 1, im])
                r, m = ar * r - ai * m + lr, ar * m + ai * r + li
            xf_ref[g, rows_f, :] = jnp.broadcast_to(ent_f, (SUBLANES, 2 * LANES))
            xb_ref[g, rows_b, :] = jnp.broadcast_to(ent_b, (SUBLANES, 2 * LANES))
            new += [r, m]
        return tuple(new)

    zero = jnp.zeros((1, LANES), F32)
    lax.fori_loop(0, nt, body, (zero,) * (2 * S5_GPT))
    for g in range(S5_GPT):
        y_ref[g] += _dot(xf_ref[g].astype(BF16), vf_ref[g]) + _dot(xb_ref[g].astype(BF16), vb_ref[g])
    for half in range(halves):
        tiles = [y_ref[g, :, half * LANES:(half + 1) * LANES] for g in range(S5_GPT)]
        for tl, tile in enumerate(_swap_blocks(tiles, blk)):
            o_ref[0, pl.ds(half * S5_GPT + tl, nc, stride=S5_CHUNK), :] = tile


def _s5_tables(lam_re, lam_im, log_dt, b_re, b_im, c_re, c_im, d_skip):
    hp = dict(precision=HIGHEST)
    dt = jnp.exp(log_dt.astype(F32))[:, :, None]
    lr, li = lam_re.astype(F32), lam_im.astype(F32)
    k = jnp.arange(S5_CHUNK + 1, dtype=F32)[:, None, None, None]
    mag = jnp.exp(k * (lr * dt)[None])
    pr = mag * jnp.cos(k * (li * dt)[None])
    pi = mag * jnp.sin(k * (li * dt)[None])
    den = lr * lr + li * li
    zr = ((pr[1] - 1.0) * lr + pi[1] * li) / den
    zi = (pi[1] * lr - (pr[1] - 1.0) * li) / den
    br, bi = b_re.astype(F32), b_im.astype(F32)
    bbr = zr[..., None] * br - zi[..., None] * bi
    bbi = zr[..., None] * bi + zi[..., None] * br
    cr, ci = c_re.astype(F32), c_im.astype(F32)
    clr = cr[None] * pr[:, :, :, None, :] - ci[None] * pi[:, :, :, None, :]
    cli = cr[None] * pi[:, :, :, None, :] + ci[None] * pr[:, :, :, None, :]
    kern = (jnp.einsum('kdgop,dgpi->kdgoi', clr, bbr, **hp)
            - jnp.einsum('kdgop,dgpi->kdgoi', cli, bbi, **hp))
    s_i = np.arange(S5_CHUNK)[:, None]
    t_i = np.arange(S5_CHUNK)[None, :]
    lag = t_i - s_i
    kf = jnp.where((lag >= 0)[:, :, None, None, None], kern[np.abs(lag), 0], 0.0)
    kb = jnp.where((lag <= 0)[:, :, None, None, None], kern[np.abs(lag), 1], 0.0)
    eye_h = jnp.eye(SSM_GROUP, dtype=F32)
    dsk = d_skip.astype(F32).reshape(SSM_GROUPS, SSM_GROUP)
    skip = (jnp.eye(S5_CHUNK, dtype=F32)[:, :, None, None, None]
            * (dsk[:, :, None] * eye_h[None])[None, None])
    tmat = (kf + kb + skip).transpose(2, 0, 4, 1, 3).reshape(SSM_GROUPS, S5_TILE, S5_TILE)
    pf_r, pf_i = pr[S5_CHUNK - 1 - np.arange(S5_CHUNK), 0], pi[S5_CHUNK - 1 - np.arange(S5_CHUNK), 0]
    pb_r, pb_i = pr[np.arange(S5_CHUNK), 1], pi[np.arange(S5_CHUNK), 1]

    def state_in(p_r, p_i, d):
        w_r = p_r[:, :, :, None] * bbr[d][None] - p_i[:, :, :, None] * bbi[d][None]
        w_i = p_r[:, :, :, None] * bbi[d][None] + p_i[:, :, :, None] * bbr[d][None]
        to_cols = lambda w: w.transpose(1, 0, 3, 2).reshape(SSM_GROUPS, S5_TILE, SSM_STATE)
        return to_cols(w_r), to_cols(w_i)

    wf_r, wf_i = state_in(pf_r, pf_i, 0)
    wb_r, wb_i = state_in(pb_r, pb_i, 1)
    wmat = jnp.concatenate([wf_r, wb_r, wf_i, wb_i], axis=-1)
    t_f = np.arange(S5_CHUNK) + 1
    t_b = S5_CHUNK - np.arange(S5_CHUNK)
    to_rows = lambda m: m.transpose(1, 3, 0, 2).reshape(SSM_GROUPS, SSM_STATE, S5_TILE)
    zeros = jnp.zeros((SSM_GROUPS, SSM_STATE, S5_TILE), F32)
    vf = jnp.concatenate([to_rows(clr[t_f, 0]), zeros, -to_rows(cli[t_f, 0]), zeros], axis=1)
    vb = jnp.concatenate([zeros, to_rows(clr[t_b, 1]), zeros, -to_rows(cli[t_b, 1])], axis=1)
    a_re = jnp.concatenate([pr[S5_CHUNK, 0], pr[S5_CHUNK, 1]], axis=-1)
    a_im = jnp.concatenate([pi[S5_CHUNK, 0], pi[S5_CHUNK, 1]], axis=-1)
    amat = jnp.concatenate([a_re[:, None], a_im[:, None], jnp.zeros((SSM_GROUPS, 6, LANES), F32)], axis=1)
    return tmat.astype(BF16), wmat.astype(BF16), vf.astype(BF16), vb.astype(BF16), amat


def _s5(u, tables, n_ctx):
    nb, l, w = u.shape
    nc = l // S5_CHUNK
    assert nc % SUBLANES == 0 and (n_ctx // S5_CHUNK) % SUBLANES == 0 and S5_GPT * SSM_GROUP == LANES
    tmat, wmat, vf, vb, amat = tables
    mat = pl.BlockSpec((S5_GPT, S5_TILE, S5_TILE), lambda j, b: (j, 0, 0))
    tok = pl.BlockSpec((1, l, LANES), lambda j, b: (b, 0, j))
    kern = functools.partial(_s5_kernel, nc=nc, ncc=n_ctx // S5_CHUNK)
    return pl.pallas_call(
        kern,
        grid=(w // LANES, nb),
        in_specs=[tok, mat, mat, mat, mat, pl.BlockSpec((S5_GPT, 8, LANES), lambda j, b: (j, 0, 0))],
        out_specs=tok,
        out_shape=jax.ShapeDtypeStruct((nb, l, w), F32),
        scratch_shapes=[pltpu.VMEM((S5_GPT, nc, S5_TILE), BF16)] + [pltpu.VMEM((S5_GPT, nc, S5_TILE), F32)] * 4,
        compiler_params=_cparams("parallel", "parallel"),
    )(u, tmat, wmat, vf, vb, amat)


def _merge_kernel(y_ref, a_ref, n_ref, gt_ref, x_ref, mods_ref, wglu_ref, wbr_ref, wout_ref, o_ref,
                  *, tm, n_ctx, n_b):
    b = pl.program_id(0)
    row0 = pl.program_id(1) * tm
    z = jax.nn.gelu(y_ref[0].astype(F32))
    glu = z * jax.nn.sigmoid(_dot(z.astype(BF16), wglu_ref[...]))
    gate = lambda k: jax.nn.sigmoid(gt_ref[0, :, k * D_MODEL:(k + 1) * D_MODEL].astype(F32))
    merged = gate(0) * _dot(glu.astype(BF16), wbr_ref[0])
    merged = merged + gate(1) * _dot(a_ref[0], wbr_ref[1])
    merged = merged + gate(2) * _dot(n_ref[0], wbr_ref[2])
    y = _dot(merged.astype(BF16), wout_ref[...])
    g1 = _mod_vec(mods_ref, 2, b, row0, tm, n_ctx, n_b)
    o_ref[0] = x_ref[0] + g1 * y


def _merge(y_ssm, gqa, na, gates, xa, mods, w_glu, w_branch, w_out, n_ctx):
    nb, l, d = xa.shape
    tm = _pick_tile(n_ctx, 256, 16)
    tok = lambda w: pl.BlockSpec((1, tm, w), lambda b, i: (b, i, 0))
    full = lambda a: pl.BlockSpec(a.shape, lambda b, i: (0,) * a.ndim)
    kern = functools.partial(_merge_kernel, tm=tm, n_ctx=n_ctx, n_b=nb)
    return pl.pallas_call(
        kern,
        grid=(nb, l // tm),
        in_specs=[tok(SSM_WIDTH), tok(SSM_WIDTH), tok(NA_WIDTH), tok(N_BRANCH * d), tok(d),
                  full(mods), full(w_glu), full(w_branch), full(w_out)],
        out_specs=tok(d),
        out_shape=jax.ShapeDtypeStruct((nb, l, d), F32),
        compiler_params=_cparams("parallel", "parallel"),
    )(y_ssm, gqa, na, gates, xa, mods, w_glu, w_branch, w_out)


def _top2_sum(v):
    hi1, lo1 = jnp.maximum(v[0], v[1]), jnp.minimum(v[0], v[1])
    hi2, lo2 = jnp.maximum(v[2], v[3]), jnp.minimum(v[2], v[3])
    return jnp.maximum(hi1, hi2) + jnp.maximum(jnp.minimum(hi1, hi2), jnp.maximum(lo1, lo2))


def _first_argmax(vals):
    best, idx = vals[0], jnp.zeros(vals[0].shape, jnp.int32)
    for j in range(1, len(vals)):
        upd = vals[j] > best
        best = jnp.where(upd, vals[j], best)
        idx = jnp.where(upd, j, idx)
    return best, idx


def _router_kernel(x_ref, mods_ref, g_ref, rw_ref, rb_ref, o_ref, *, tm, n_ctx, n_b):
    b = pl.program_id(0)
    row0 = pl.program_id(1) * tm
    sh = _mod_vec(mods_ref, 3, b, row0, tm, n_ctx, n_b)
    sc = _mod_vec(mods_ref, 4, b, row0, tm, n_ctx, n_b)
    h = _norm_mod(x_ref[0], g_ref[...], sc, sh)
    logits = lax.dot_general(rw_ref[...], h, (((1,), (1,)), ((), ())), precision=HIGHEST,
                             preferred_element_type=F32)
    rows = [logits[e:e + 1, :] for e in range(N_EXPERTS)]
    mx = functools.reduce(jnp.maximum, rows)
    ex = [jnp.exp(r - mx) for r in rows]
    den = functools.reduce(lambda a, c: a + c, ex)
    probs = [e / den for e in ex]
    sel = [probs[e] + rb_ref[e:e + 1, :] for e in range(N_EXPERTS)]
    grp = [sel[g * EXPERTS_PER_GROUP:(g + 1) * EXPERTS_PER_GROUP] for g in range(N_EXPERT_GROUPS)]
    _, best = _first_argmax([_top2_sum(v) for v in grp])
    pick = lambda table, j: functools.reduce(
        lambda a, g: jnp.where(best == g, table[g * EXPERTS_PER_GROUP + j], a),
        range(1, N_EXPERT_GROUPS), table[j])
    sel_g = [pick(sel, j) for j in range(EXPERTS_PER_GROUP)]
    prob_g = [pick(probs, j) for j in range(EXPERTS_PER_GROUP)]
    _, i1 = _first_argmax(sel_g)
    _, i2 = _first_argmax([jnp.where(i1 == j, -jnp.inf, sel_g[j]) for j in range(EXPERTS_PER_GROUP)])
    at = lambda idx: functools.reduce(lambda a, j: jnp.where(idx == j, prob_g[j], a),
                                      range(1, EXPERTS_PER_GROUP), prob_g[0])
    w1, w2 = at(i1), at(i2)
    tot = w1 + w2
    w1, w2 = w1 / tot, w2 / tot
    out = []
    for e in range(N_EXPERTS):
        g, j = divmod(e, EXPERTS_PER_GROUP)
        c = jnp.where(i1 == j, w1, 0.0) + jnp.where(i2 == j, w2, 0.0)
        out.append(jnp.where(best == g, c, 0.0))
    o_ref[0] = jnp.concatenate(out, axis=0)


def _router(xa, mods, g, router_w, router_b, n_ctx):
    nb, l, d = xa.shape
    tm = _pick_tile(n_ctx, 256, LANES)
    kern = functools.partial(_router_kernel, tm=tm, n_ctx=n_ctx, n_b=nb)
    comb_t = pl.pallas_call(
        kern,
        grid=(nb, l // tm),
        in_specs=[pl.BlockSpec((1, tm, d), lambda b, i: (b, i, 0)),
                  pl.BlockSpec(mods.shape, lambda b, i: (0, 0)),
                  pl.BlockSpec((1, d), lambda b, i: (0, 0)),
                  pl.BlockSpec((N_EXPERTS, d), lambda b, i: (0, 0)),
                  pl.BlockSpec((N_EXPERTS, 1), lambda b, i: (0, 0))],
        out_specs=pl.BlockSpec((1, N_EXPERTS, tm), lambda b, i: (b, 0, i)),
        out_shape=jax.ShapeDtypeStruct((nb, N_EXPERTS, l), F32),
        compiler_params=_cparams("parallel", "parallel"),
    )(xa, mods, g.reshape(1, d), router_w.T.astype(F32), router_b.reshape(N_EXPERTS, 1).astype(F32))
    return comb_t


def _moe_kernel(cnt_ref, x_ref, combt_ref, mods_ref, g_ref, wg_ref, wu_ref, wd_ref, o_ref,
                h_ref, acc_ref, rankt_ref, *, tm, n_ctx, n_b):
    b = pl.program_id(0)
    i = pl.program_id(1)
    ep = pl.program_id(2)
    row0 = i * tm

    @pl.when(ep == 0)
    def _():
        sh = _mod_vec(mods_ref, 3, b, row0, tm, n_ctx, n_b)
        sc = _mod_vec(mods_ref, 4, b, row0, tm, n_ctx, n_b)
        h_ref[...] = _norm_mod(x_ref[0], g_ref[...], sc, sh).astype(BF16)
        acc_ref[...] = jnp.zeros(acc_ref.shape, F32)
        routed = jnp.where(combt_ref[0] != 0.0, 1.0, 0.0)
        pos = lax.broadcasted_iota(jnp.int32, (1, tm), 1)
        count, shift = routed, 1
        while shift < tm:
            count = count + jnp.where(pos >= shift, pltpu.roll(count, shift, 1), 0.0)
            shift *= 2
        rankt_ref[...] = count - routed

    base = ((b * pl.num_programs(1) + i) * MOE_PAIRS + ep) * MOE_PAIR
    n_slots = functools.reduce(jnp.maximum, [cnt_ref[base + k] for k in range(MOE_PAIR)])
    slot_r = lax.broadcasted_iota(jnp.int32, (MOE_CAP, 1), 0).astype(F32)
    rows = [(rankt_ref[pl.ds(ep * MOE_PAIR + k, 1), :], combt_ref[0, pl.ds(ep * MOE_PAIR + k, 1), :])
            for k in range(MOE_PAIR)]

    def body(j, carry):
        first = (j * MOE_CAP).astype(F32)
        sel = [(rank == first + slot_r) & (w != 0.0) for rank, w in rows]
        one_hot = jnp.concatenate([jnp.where(m, 1.0, 0.0).astype(BF16) for m in sel], axis=0)
        xc = _dot(one_hot, h_ref[...]).astype(BF16)
        ys = []
        for k in range(MOE_PAIR):
            xk = xc[k * MOE_CAP:(k + 1) * MOE_CAP]
            gate = _dot(xk, wg_ref[k])
            act = (gate * jax.nn.sigmoid(gate)) * _dot(xk, wu_ref[k])
            y = _dot(act.astype(BF16), wd_ref[k])
            weight = jnp.sum(jnp.where(sel[k], rows[k][1], 0.0), axis=1, keepdims=True)
            ys.append((y * weight).astype(BF16))
        acc_ref[...] += _dot_tn(one_hot, jnp.concatenate(ys, axis=0))
        return carry

    lax.fori_loop(0, (n_slots + MOE_CAP - 1) // MOE_CAP, body, 0)

    @pl.when(ep == MOE_PAIRS - 1)
    def _():
        g2 = _mod_vec(mods_ref, 5, b, row0, tm, n_ctx, n_b)
        o_ref[0] = x_ref[0] + g2 * acc_ref[...]


def _moe(xa, comb_t, mods, g, w_gate, w_up, w_down, n_ctx):
    nb, l, d = xa.shape
    tm = _pick_tile(l, 1408, LANES)
    nt = l // tm
    counts = (comb_t != 0.0).reshape(nb, N_EXPERTS, nt, tm).sum(axis=-1).astype(jnp.int32)
    counts = counts.transpose(0, 2, 1).reshape(-1)
    kern = functools.partial(_moe_kernel, tm=tm, n_ctx=n_ctx, n_b=nb)
    grid_spec = pltpu.PrefetchScalarGridSpec(
        num_scalar_prefetch=1,
        grid=(nb, nt, MOE_PAIRS),
        in_specs=[pl.BlockSpec((1, tm, d), lambda b, i, e, cnt: (b, i, 0)),
                  pl.BlockSpec((1, N_EXPERTS, tm), lambda b, i, e, cnt: (b, 0, i)),
                  pl.BlockSpec(mods.shape, lambda b, i, e, cnt: (0, 0)),
                  pl.BlockSpec((1, d), lambda b, i, e, cnt: (0, 0)),
                  pl.BlockSpec((MOE_PAIR, d, EXPERT_FF), lambda b, i, e, cnt: (e, 0, 0)),
                  pl.BlockSpec((MOE_PAIR, d, EXPERT_FF), lambda b, i, e, cnt: (e, 0, 0)),
                  pl.BlockSpec((MOE_PAIR, EXPERT_FF, d), lambda b, i, e, cnt: (e, 0, 0))],
        out_specs=pl.BlockSpec((1, tm, d), lambda b, i, e, cnt: (b, i, 0)),
        scratch_shapes=[pltpu.VMEM((tm, d), BF16), pltpu.VMEM((tm, d), F32), pltpu.VMEM((N_EXPERTS, tm), F32)],
    )
    return pl.pallas_call(
        kern,
        grid_spec=grid_spec,
        out_shape=jax.ShapeDtypeStruct((nb, l, d), F32),
        compiler_params=_cparams("parallel", "parallel", "arbitrary"),
    )(counts, xa, comb_t, mods, g.reshape(1, d), w_gate, w_up, w_down)


def _final_norm_kernel(x_ref, g_ref, o_ref):
    x = x_ref[0]
    ms = jnp.mean(x * x, axis=-1, keepdims=True)
    o_ref[0] = x * lax.rsqrt(ms + NORM_EPS) * g_ref[...]


def _final_norm(xa, g, n_ctx):
    nb, l, d = xa.shape
    tm = _pick_tile(n_ctx, 256, 8)
    off = n_ctx // tm
    return pl.pallas_call(
        _final_norm_kernel,
        grid=(nb, (l - n_ctx) // tm),
        in_specs=[pl.BlockSpec((1, tm, d), lambda b, i: (b, i + off, 0)),
                  pl.BlockSpec((1, d), lambda b, i: (0, 0))],
        out_specs=pl.BlockSpec((1, tm, d), lambda b, i: (b, i, 0)),
        out_shape=jax.ShapeDtypeStruct((nb, l - n_ctx, d), F32),
        compiler_params=_cparams("parallel", "parallel"),
    )(xa, g.reshape(1, d).astype(F32))


def _in_proj_weights(w_in):
    w_u, w_qa, w_rest = jnp.split(w_in, [SSM_WIDTH, SSM_WIDTH + GQA_HEADS * HEAD_DIM], axis=-1)
    per_kv = GQA_HEADS // GQA_KV_HEADS
    zero = jnp.zeros(w_in.shape[:-1] + (HEAD_DIM,), w_in.dtype)
    tiles = []
    for h in range(GQA_HEADS):
        w_h = w_qa[..., h * HEAD_DIM:(h + 1) * HEAD_DIM]
        tiles += [w_h, zero] if h // per_kv == 0 else [zero, w_h]
    return jnp.concatenate([w_u] + tiles + [w_rest], axis=-1).astype(BF16)


def kernel(x, c, ctx, c_ctx, w_mod, b_mod, norm1_g, norm2_g, w_in, ssm_lam_re, ssm_lam_im, ssm_log_dt, ssm_b_re, ssm_b_im, ssm_c_re, ssm_c_im, ssm_d, ssm_w_glu, gqa_q_norm_g, gqa_k_norm_g, na_rpb, w_branch, w_out, router_w, router_b, moe_w_gate, moe_w_up, moe_w_down, final_norm_g):
    nb, n_lat, d = x.shape
    n_ctx = ctx.shape[1]
    depth = w_mod.shape[0]
    assert d == D_MODEL and nb < 8 and GQA_KV_HEADS == 2 and sum(IN_WIDTHS) % LANES == 0
    assert n_ctx % S5_CHUNK == 0 and n_lat % NA_BLOCK == 0

    xa = jnp.concatenate([ctx, x], axis=1).astype(F32)
    cond = jnp.zeros((8, d), F32).at[:nb].set(c.astype(F32)).at[nb].set(c_ctx.astype(F32))
    mods = _modvecs(cond, w_mod.astype(F32), b_mod.astype(F32))
    cos_t, sin_t = _rope_tables(n_ctx, n_lat)
    w_cat = _in_proj_weights(w_in)
    s5_tables = jax.vmap(_s5_tables)(ssm_lam_re, ssm_lam_im, ssm_log_dt, ssm_b_re, ssm_b_im, ssm_c_re, ssm_c_im,
                                     ssm_d)
    bm, var_idx = _na_bias_tables(na_rpb.reshape((depth * NA_HEADS,) + na_rpb.shape[2:]), n_lat)
    bm = bm.reshape((bm.shape[0], depth, NA_HEADS) + bm.shape[2:])
    w_glu, w_br, w_o = ssm_w_glu.astype(BF16), w_branch.astype(BF16), w_out.astype(BF16)
    w_g, w_u, w_d = moe_w_gate.astype(BF16), moe_w_up.astype(BF16), moe_w_down.astype(BF16)
    n1, n2 = norm1_g.astype(F32), norm2_g.astype(F32)

    for layer in range(depth):
        u, qp, kp, vp, qn, kn, vn, gates = _in_proj(xa, mods[layer], n1[layer], w_cat[layer], cos_t, sin_t,
                                                    gqa_q_norm_g[layer], gqa_k_norm_g[layer], n_ctx)
        y_ssm = _s5(u, [t[layer] for t in s5_tables], n_ctx)
        gqa = _gqa_attention(qp, kp, vp, n_ctx)
        na = _na_attention(qn, kn, vn, bm[:, layer], var_idx, n_ctx)
        xa = _merge(y_ssm, gqa, na, gates, xa, mods[layer], w_glu[layer], w_br[layer], w_o[layer], n_ctx)
        comb_t = _router(xa, mods[layer], n2[layer], router_w, router_b, n_ctx)
        xa = _moe(xa, comb_t, mods[layer], n2[layer], w_g[layer], w_u[layer], w_d[layer], n_ctx)
    return _final_norm(xa, final_norm_g, n_ctx)
```
